```python
import math
import jax, jax.numpy as jnp
from jax import lax
import numpy as np

D_MODEL = 1024
BATCH = 2
SEQ = 16384
DEPTH = 4
DEC_BATCH = 8
DEC_SEQ = 8192
PAST_LEN = 128

N_META = 16
BLOCK = 128
WINDOW = 128
PAD = BLOCK - N_META
HEAD_DIM = 64
ATT_HEADS = 8
ATT_KV_HEADS = 2
RET_HEADS = 4
MLSTM_HEADS = 4
ATT_W = ATT_HEADS * HEAD_DIM
KV_W = ATT_KV_HEADS * HEAD_DIM
RET_W = RET_HEADS * HEAD_DIM
MLSTM_W = MLSTM_HEADS * HEAD_DIM
MIX_W = ATT_W + RET_W + MLSTM_W
N_GATES = 4 * MLSTM_HEADS
PROJ_SIZES = (ATT_W, KV_W, KV_W, RET_W, RET_W, RET_W, RET_W, MLSTM_W, MLSTM_W, MLSTM_W, MLSTM_W, N_GATES)
PROJ_W = ATT_W + 2 * KV_W + 4 * RET_W + 4 * MLSTM_W + N_GATES
D_FF = 2816
CONV_W = 3
EPS = 1e-6

kernel_name = 'hymba_style_bidir_hybrid_encoder'

f32 = jnp.float32


def rms_norm(x, w):
    xf = x.astype(f32)
    y = xf * lax.rsqrt(jnp.mean(xf * xf, axis=-1, keepdims=True) + EPS)
    return (y * w.astype(f32)).astype(x.dtype)


def head_norm(x, w):
    B, H, Lp, d = x.shape
    mu = jnp.mean(x, axis=-1, keepdims=True)
    xc = x - mu
    y = xc * lax.rsqrt(jnp.mean(xc * xc, axis=-1, keepdims=True) + EPS)
    return y.transpose(0, 2, 1, 3).reshape(B, Lp, H * d) * w.astype(f32)


def dwconv3(x, w):
    xp = jnp.pad(x, ((0, 0), (1, 1), (0, 0)))
    return xp[:, :-2] * w[0] + xp[:, 1:-1] * w[1] + xp[:, 2:] * w[2]


def _flip(a):
    return jnp.flip(a, axis=2)


def _neighbour_blocks(a, nb):
    B = a.shape[0]
    ab = a.reshape((B, nb, BLOCK) + a.shape[2:])
    z = jnp.zeros_like(ab[:, :1])
    ap = jnp.concatenate([z, ab, z], axis=1)
    return jnp.concatenate([ap[:, :-2], ap[:, 1:-1], ap[:, 2:]], axis=2)


def windowed_attention(q, k, v, sink):
    B, Lp, H, dh = q.shape
    Hkv = k.shape[2]
    G = H // Hkv
    nb = Lp // BLOCK
    scale = dh ** -0.5
    slopes = jnp.exp2(-8.0 * jnp.arange(1, H + 1, dtype=f32) / H).reshape(Hkv, G)
    qb = q.reshape(B, nb, BLOCK, Hkv, G, dh)
    kb = _neighbour_blocks(k, nb)
    vb = _neighbour_blocks(v, nb)
    qpos = jnp.arange(Lp).reshape(nb, BLOCK)
    kpos = (jnp.arange(nb)[:, None] - 1) * BLOCK + jnp.arange(3 * BLOCK)[None, :]
    dist = jnp.abs(qpos[:, :, None] - kpos[:, None, :])
    band = (dist <= WINDOW) & (kpos[:, None, :] >= BLOCK) & (kpos[:, None, :] < Lp)
    s_band = jnp.einsum('bnqkgd,bnskd->bnkgqs', qb, kb, preferred_element_type=f32) * scale
    s_band = s_band - slopes[:, :, None, None] * dist[:, None, None].astype(f32)
    s_band = jnp.where(band[:, None, None], s_band, -jnp.inf)
    km = k[:, PAD:BLOCK]
    vm = v[:, PAD:BLOCK]
    s_meta = jnp.einsum('bnqkgd,bmkd->bnkgqm', qb, km, preferred_element_type=f32) * scale
    sink_l = sink.astype(f32).reshape(Hkv, G)[:, :, None]
    mx = jnp.maximum(jnp.maximum(jnp.max(s_band, axis=-1), jnp.max(s_meta, axis=-1)), sink_l)
    p_band = jnp.exp(s_band - mx[..., None])
    p_meta = jnp.exp(s_meta - mx[..., None])
    denom = jnp.sum(p_band, axis=-1) + jnp.sum(p_meta, axis=-1) + jnp.exp(sink_l - mx)
    out = (jnp.einsum('bnkgqs,bnskd->bnqkgd', p_band.astype(v.dtype), vb, preferred_element_type=f32)
           + jnp.einsum('bnkgqm,bmkd->bnqkgd', p_meta.astype(v.dtype), vm, preferred_element_type=f32))
    out = out / jnp.moveaxis(denom, 4, 2)[..., None]
    return out.reshape(B, Lp, H * dh).astype(q.dtype)


def _retention_scan(q, k, v, log_gamma):
    B, H, Lp, d = q.shape
    nc = Lp // BLOCK
    q = q.reshape(B, H, nc, BLOCK, d)
    k = k.reshape(B, H, nc, BLOCK, d)
    v = v.reshape(B, H, nc, BLOCK, d)
    idx = jnp.arange(BLOCK, dtype=f32)
    diff = idx[:, None] - idx[None, :]
    decay = jnp.where(diff >= 0, jnp.exp(log_gamma[:, None, None] * jnp.maximum(diff, 0.0)), 0.0)
    scores = jnp.einsum('bhnqd,bhnsd->bhnqs', q, k) * decay[:, None]
    o = jnp.einsum('bhnqs,bhnse->bhnqe', scores, v)
    lg = log_gamma[:, None]
    k_end = k * jnp.exp(lg * (BLOCK - 1 - idx))[:, None, :, None]
    chunk_kv = jnp.einsum('bhnsd,bhnse->nbhde', k_end, v)
    gamma_blk = jnp.exp(log_gamma * BLOCK)[:, None, None]

    def step(R, kv):
        return R * gamma_blk + kv, R

    _, R_prev = lax.scan(step, jnp.zeros_like(chunk_kv[0]), chunk_kv)
    q_dec = q * jnp.exp(lg * (idx + 1.0))[:, None, :, None]
    o = o + jnp.einsum('bhnqd,nbhde->bhnqe', q_dec, R_prev)
    return o.reshape(B, H, Lp, d)


def _mlstm_scan(q, k, v, log_i, log_f):
    B, H, Lp, d = q.shape
    nc = Lp // BLOCK
    q = q.reshape(B, H, nc, BLOCK, d)
    k = k.reshape(B, H, nc, BLOCK, d)
    v = v.reshape(B, H, nc, BLOCK, d)
    li = log_i.reshape(B, H, nc, BLOCK)
    b = jnp.cumsum(log_f.reshape(B, H, nc, BLOCK), axis=-1)
    tril = jnp.tril(jnp.ones((BLOCK, BLOCK), dtype=bool))
    d_log = jnp.where(tril, b[..., :, None] - b[..., None, :] + li[..., None, :], -jnp.inf)
    b_last = b[..., -1]
    w_end = b_last[..., None] - b + li
    m_loc = jnp.max(w_end, axis=-1)
    e_end = jnp.exp(w_end - m_loc[..., None])
    kv_loc = jnp.einsum('bhnc,bhncd,bhnce->nbhde', e_end, k, v)
    n_loc = jnp.einsum('bhnc,bhncd->nbhd', e_end, k)

    def step(carry, inp):
        S, nv, m = carry
        kv_c, n_c, m_c, bl = inp
        m_new = jnp.maximum(bl + m, m_c)
        a = jnp.exp(bl + m - m_new)
        c = jnp.exp(m_c - m_new)
        S_new = a[..., None, None] * S + c[..., None, None] * kv_c
        n_new = a[..., None] * nv + c[..., None] * n_c
        return (S_new, n_new, m_new), (S, nv, m)

    init = (jnp.zeros((B, H, d, d), f32), jnp.zeros((B, H, d), f32), jnp.zeros((B, H), f32))
    xs = (kv_loc, n_loc, jnp.moveaxis(m_loc, 2, 0), jnp.moveaxis(b_last, 2, 0))
    _, (S_prev, n_prev, m_prev) = lax.scan(step, init, xs)
    inter = b + jnp.moveaxis(m_prev, 0, 2)[..., None]
    m_t = jnp.maximum(jnp.max(d_log, axis=-1), inter)
    a_inter = jnp.exp(inter - m_t)
    w = jnp.exp(d_log - m_t[..., None]) * jnp.einsum('bhnqd,bhnsd->bhnqs', q, k)
    num = jnp.einsum('bhnqs,bhnse->bhnqe', w, v) + a_inter[..., None] * jnp.einsum('bhnqd,nbhde->bhnqe', q, S_prev)
    den = jnp.sum(w, axis=-1) + a_inter * jnp.einsum('bhnqd,nbhd->bhnq', q, n_prev)
    h = num / jnp.maximum(jnp.abs(den), jnp.exp(-m_t))[..., None]
    return h.reshape(B, H, Lp, d)


def hybrid_mixer(h, w_in, q_norm_w, k_norm_w, sink, ret_decay_logit, ret_norm_w,
                 mlstm_conv_w, mlstm_gate_b, mlstm_norm_w, w_out):
    B, L, _ = h.shape
    Lp = L + PAD
    proj = jnp.pad(h @ w_in, ((0, 0), (PAD, 0), (0, 0)))
    splits = np.cumsum(PROJ_SIZES)[:-1].tolist()
    aq, ak, av, rq, rk, rv, rg, mq, mk, mv, mo, mgate = jnp.split(proj, splits, axis=-1)

    def heads(a, n):
        return a.reshape(B, Lp, n, HEAD_DIM).transpose(0, 2, 1, 3).astype(f32)

    aq = rms_norm(aq.reshape(B, Lp, ATT_HEADS, HEAD_DIM), q_norm_w)
    ak = rms_norm(ak.reshape(B, Lp, ATT_KV_HEADS, HEAD_DIM), k_norm_w)
    av = av.reshape(B, Lp, ATT_KV_HEADS, HEAD_DIM)
    att = windowed_attention(aq, ak, av, sink)

    rq_h = heads(rq, RET_HEADS)
    rk_h = heads(rk, RET_HEADS) * (HEAD_DIM ** -0.5)
    rv_h = heads(rv, RET_HEADS)
    log_gamma = jax.nn.log_sigmoid(ret_decay_logit.astype(f32))
    ro = (_retention_scan(rq_h, rk_h, rv_h, log_gamma[0])
          + _flip(_retention_scan(_flip(rq_h), _flip(rk_h), _flip(rv_h), log_gamma[1])))
    ret = (jax.nn.silu(rg.astype(f32)) * head_norm(ro, ret_norm_w)).astype(h.dtype)

    mqk = jax.nn.silu(dwconv3(jnp.concatenate([mq, mk], axis=-1), mlstm_conv_w))
    mq, mk = jnp.split(mqk, 2, axis=-1)
    not_pad = (jnp.arange(Lp) >= PAD)[None, :, None]
    mk = jnp.where(not_pad, mk, jnp.zeros_like(mk))
    g = (mgate.astype(f32) + mlstm_gate_b.astype(f32)).reshape(B, Lp, 4, MLSTM_HEADS).transpose(2, 0, 3, 1)
    mq_h = heads(mq, MLSTM_HEADS)
    mk_h = heads(mk, MLSTM_HEADS) * (HEAD_DIM ** -0.5)
    mv_h = heads(mv, MLSTM_HEADS)
    hm = (_mlstm_scan(mq_h, mk_h, mv_h, g[0], jax.nn.log_sigmoid(g[1]))
          + _flip(_mlstm_scan(_flip(mq_h), _flip(mk_h), _flip(mv_h), _flip(g[2]), _flip(jax.nn.log_sigmoid(g[3])))))
    ml = (jax.nn.sigmoid(mo.astype(f32)) * head_norm(hm, mlstm_norm_w)).astype(h.dtype)

    mixed = jnp.concatenate([att, ret, ml], axis=-1)[:, PAD:]
    return mixed @ w_out


def conv_ffn(h, w_up, conv_w, w_down):
    u = dwconv3(h @ w_up, conv_w)
    gate, val = jnp.split(u, 2, axis=-1)
    return (jax.nn.silu(gate) * val) @ w_down


def trunk(x, meta_tokens, norm1_w, w_in, attn_q_norm_w, attn_k_norm_w, attn_sink, ret_decay_logit,
          ret_norm_w, mlstm_conv_w, mlstm_gate_b, mlstm_norm_w, w_out, norm2_w, ffn_up, ffn_conv_w, ffn_down):
    B = x.shape[0]
    meta = jnp.broadcast_to(meta_tokens[None].astype(x.dtype), (B, N_META, x.shape[-1]))
    h = jnp.concatenate([meta, x], axis=1)
    for l in range(DEPTH):
        h = h + hybrid_mixer(rms_norm(h, norm1_w[l]), w_in[l], attn_q_norm_w[l], attn_k_norm_w[l],
                             attn_sink[l], ret_decay_logit[l], ret_norm_w[l], mlstm_conv_w[l],
                             mlstm_gate_b[l], mlstm_norm_w[l], w_out[l])
        h = h + conv_ffn(rms_norm(h, norm2_w[l]), ffn_up[l], ffn_conv_w[l], ffn_down[l])
    return h[:, N_META:]


def setup_inputs(seed: int = 0) -> dict:
    key = jax.random.key(seed)
    ks = jax.random.split(key, 20)

    def nrm(k, shape, s):
        return jax.random.normal(k, shape, f32) * s

    gamma0 = 1.0 - 2.0 ** (-5.0 - np.arange(RET_HEADS))
    logit0 = (np.log(gamma0) - np.log1p(-gamma0)).astype(np.float32)
    f_bias = np.linspace(3.0, 6.0, MLSTM_HEADS)
    zeros_h = np.zeros(MLSTM_HEADS)
    gate_base = np.concatenate([zeros_h, f_bias, zeros_h, f_bias]).astype(np.float32)
    return {
        'x_prompt': nrm(ks[0], (BATCH, SEQ, D_MODEL), 1.0),
        'x_sample': nrm(ks[1], (DEC_BATCH, DEC_SEQ, D_MODEL), 1.0),
        'meta_tokens': nrm(ks[2], (N_META, D_MODEL), 1.0),
        'norm1_w': 1.0 + nrm(ks[3], (DEPTH, D_MODEL), 0.02),
        'w_in': nrm(ks[4], (DEPTH, D_MODEL, PROJ_W), D_MODEL ** -0.5),
        'attn_q_norm_w': 1.0 + nrm(ks[5], (DEPTH, HEAD_DIM), 0.02),
        'attn_k_norm_w': 1.0 + nrm(ks[6], (DEPTH, HEAD_DIM), 0.02),
        'attn_sink': nrm(ks[7], (DEPTH, ATT_HEADS), 0.5),
        'ret_decay_logit': jnp.asarray(logit0)[None, None, :] + nrm(ks[8], (DEPTH, 2, RET_HEADS), 0.1),
        'ret_norm_w': 1.0 + nrm(ks[9], (DEPTH, RET_W), 0.02),
        'mlstm_conv_w': nrm(ks[10], (DEPTH, CONV_W, 2 * MLSTM_W), CONV_W ** -0.5),
        'mlstm_gate_b': jnp.asarray(gate_base)[None, :] + nrm(ks[11], (DEPTH, N_GATES), 0.1),
        'mlstm_norm_w': 1.0 + nrm(ks[12], (DEPTH, MLSTM_W), 0.02),
        'w_out': nrm(ks[13], (DEPTH, MIX_W, D_MODEL), MIX_W ** -0.5),
        'norm2_w': 1.0 + nrm(ks[14], (DEPTH, D_MODEL), 0.02),
        'ffn_up': nrm(ks[15], (DEPTH, D_MODEL, 2 * D_FF), D_MODEL ** -0.5),
        'ffn_conv_w': nrm(ks[16], (DEPTH, CONV_W, 2 * D_FF), CONV_W ** -0.5),
        'ffn_down': nrm(ks[17], (DEPTH, D_FF, D_MODEL), D_FF ** -0.5),
    }


def reference(x_prompt, x_sample, meta_tokens, norm1_w, w_in, attn_q_norm_w, attn_k_norm_w, attn_sink,
              ret_decay_logit, ret_norm_w, mlstm_conv_w, mlstm_gate_b, mlstm_norm_w, w_out, norm2_w,
              ffn_up, ffn_conv_w, ffn_down):
    y_prompt = trunk(x_prompt, meta_tokens, norm1_w, w_in, attn_q_norm_w, attn_k_norm_w, attn_sink,
                     ret_decay_logit, ret_norm_w, mlstm_conv_w, mlstm_gate_b, mlstm_norm_w, w_out,
                     norm2_w, ffn_up, ffn_conv_w, ffn_down)
    y_sample = trunk(x_sample, meta_tokens, norm1_w, w_in, attn_q_norm_w, attn_k_norm_w, attn_sink,
                     ret_decay_logit, ret_norm_w, mlstm_conv_w, mlstm_gate_b, mlstm_norm_w, w_out,
                     norm2_w, ffn_up, ffn_conv_w, ffn_down)
    return (y_prompt, y_sample)
```

```python
import functools
import math

import jax
import jax.numpy as jnp
from jax import lax
from jax.experimental import pallas as pl
from jax.experimental.pallas import tpu as pltpu

f32 = jnp.float32
bf16 = jnp.bfloat16

D_MODEL = 1024
DEPTH = 4
N_META = 16
BLOCK = 128
PAD = BLOCK - N_META
HEAD_DIM = 64
ATT_HEADS = 8
ATT_KV_HEADS = 2
ATT_W = 512
KV_W = 128
HW = 256
N_GATES = 16
MAIN_W = 2816
REST_W = 2048
ATT_OUT_W = 1024
D_FF = 2816
FF_CHUNK = 256
EPS = 1e-6
NEG = -1e30
VMEM_LIMIT = 56 * 1024 * 1024


def _row_tile(lp):
    nb = lp // BLOCK
    best = 1
    for t in range(1, 7):
        if nb % t == 0:
            best = t
    return best * BLOCK


def _sigmoid(x):
    return 1.0 / (1.0 + jnp.exp(-x))


def _log_sigmoid(x):
    return jnp.minimum(x, 0.0) - jnp.log(1.0 + jnp.exp(-jnp.abs(x)))


def _seg64_mean(x, lo):
    a = jnp.sum(jnp.where(lo, x, 0.0), axis=-1, keepdims=True)
    b = jnp.sum(jnp.where(lo, 0.0, x), axis=-1, keepdims=True)
    return jnp.where(lo, a, b) * (1.0 / HEAD_DIM)


def _head_layernorm(x, w):
    lo = lax.broadcasted_iota(jnp.int32, (1, BLOCK), 1) < HEAD_DIM
    outs = []
    for c in range(x.shape[1] // BLOCK):
        xc = x[:, c * BLOCK:(c + 1) * BLOCK]
        mu = _seg64_mean(xc, lo)
        d = xc - mu
        var = _seg64_mean(d * d, lo)
        outs.append(d * lax.rsqrt(var + EPS))
    return jnp.concatenate(outs, axis=-1) * w


def _bcast_heads(cols):
    lane = lax.broadcasted_iota(jnp.int32, (1, HW), 1)
    return jnp.where(lane < 64, cols[0],
                     jnp.where(lane < 128, cols[1], jnp.where(lane < 192, cols[2], cols[3])))


def _head_rows(vals):
    row = lax.broadcasted_iota(jnp.int32, (HW, 1), 0)
    return jnp.where(row < 64, vals[0],
                     jnp.where(row < 128, vals[1], jnp.where(row < 192, vals[2], vals[3])))


def _block_stack(x_b):
    lane = lax.broadcasted_iota(jnp.int32, (1, HW), 1)
    parts = []
    for h in range(4):
        keep = (lane >= h * HEAD_DIM) & (lane < (h + 1) * HEAD_DIM)
        parts.append(jnp.where(keep, x_b, jnp.zeros_like(x_b)))
    return jnp.concatenate(parts, axis=0)


def _bd_mask():
    r = lax.broadcasted_iota(jnp.int32, (HW, HW), 0) >> 6
    c = lax.broadcasted_iota(jnp.int32, (HW, HW), 1) >> 6
    return r == c


def _dot(a, b):
    return jnp.dot(a, b, preferred_element_type=f32)


def _dot_nt(a, b):
    return lax.dot_general(a, b, (((1,), (1,)), ((), ())), preferred_element_type=f32)


def _split3(x):
    p1 = x.astype(bf16)
    r1 = x - p1.astype(f32)
    p2 = r1.astype(bf16)
    p3 = (r1 - p2.astype(f32)).astype(bf16)
    return p1, p2, p3


def _dot01_left(m01, x):
    p1, p2, p3 = _split3(x)
    return (_dot(m01, p3) + _dot(m01, p2)) + _dot(m01, p1)


def _dot01_right(x, m01):
    p1, p2, p3 = _split3(x)
    return (_dot(p3, m01) + _dot(p2, m01)) + _dot(p1, m01)


def _proj_kernel(h_ref, n1_ref, w_ref, wg_ref, qkw_ref, att_ref, rest_ref, gate_ref):
    x = h_ref[0]
    ms = jnp.mean(x * x, axis=-1, keepdims=True)
    hn = ((x * lax.rsqrt(ms + EPS)) * n1_ref[...]).astype(bf16)
    pa = _dot(hn, w_ref[:, 0:ATT_W + 2 * KV_W])
    lo = lax.broadcasted_iota(jnp.int32, (1, BLOCK), 1) < HEAD_DIM
    pieces = []
    for c in range(5):
        xc = pa[:, c * BLOCK:(c + 1) * BLOCK]
        msq = _seg64_mean(xc * xc, lo)
        pieces.append((xc * lax.rsqrt(msq + EPS)) * qkw_ref[c:c + 1, :])
    k = pieces[4]
    v = pa[:, ATT_W + KV_W:ATT_W + 2 * KV_W]
    k_sw = pltpu.roll(k, HEAD_DIM, 1)
    v_sw = pltpu.roll(v, HEAD_DIM, 1)
    out = pieces[:4] + [jnp.where(lo, k, k_sw), jnp.where(lo, k_sw, k),
                        jnp.where(lo, v, v_sw), jnp.where(lo, v_sw, v)]
    att_ref[0] = jnp.concatenate(out, axis=-1).astype(bf16)
    rest_ref[0] = _dot(hn, w_ref[:, ATT_W + 2 * KV_W:MAIN_W])
    gate_ref[0] = _dot(hn, wg_ref[...])


def _project(h, n1, w_main, w_gate, qkw, tm):
    B, Lp, _ = h.shape
    grid = (B, Lp // tm)
    return pl.pallas_call(
        _proj_kernel,
        grid=grid,
        in_specs=[
            pl.BlockSpec((1, tm, D_MODEL), lambda b, i: (b, i, 0)),
            pl.BlockSpec((1, D_MODEL), lambda b, i: (0, 0)),
            pl.BlockSpec((D_MODEL, MAIN_W), lambda b, i: (0, 0)),
            pl.BlockSpec((D_MODEL, BLOCK), lambda b, i: (0, 0)),
            pl.BlockSpec((8, BLOCK), lambda b, i: (0, 0)),
        ],
        out_specs=[
            pl.BlockSpec((1, tm, ATT_OUT_W), lambda b, i: (b, i, 0)),
            pl.BlockSpec((1, tm, REST_W), lambda b, i: (b, i, 0)),
            pl.BlockSpec((1, tm, BLOCK), lambda b, i: (b, i, 0)),
        ],
        out_shape=[
            jax.ShapeDtypeStruct((B, Lp, ATT_OUT_W), bf16),
            jax.ShapeDtypeStruct((B, Lp, REST_W), f32),
            jax.ShapeDtypeStruct((B, Lp, BLOCK), f32),
        ],
        compiler_params=pltpu.CompilerParams(
            dimension_semantics=("parallel", "parallel"), vmem_limit_bytes=VMEM_LIMIT),
        name="proj",
    )(h, n1, w_main, w_gate, qkw)


def _attn_kernel(q_ref, kp_ref, kc_ref, kn_ref, vp_ref, vc_ref, vn_ref, km_ref, vm_ref,
                 bias_ref, sink_ref, o_ref):
    j = pl.program_id(1)
    nb = pl.num_programs(1)
    q = q_ref[0]
    col = lax.broadcasted_iota(jnp.int32, (1, 4 * BLOCK), 1)
    zero = jnp.float32(0.0)
    pen = jnp.where(col < BLOCK, jnp.where(j >= 2, zero, NEG),
                    jnp.where(col < 2 * BLOCK, jnp.where(j >= 1, zero, NEG),
                              jnp.where(col < 3 * BLOCK, jnp.where(j + 1 < nb, zero, NEG), zero)))
    lo = lax.broadcasted_iota(jnp.int32, (1, BLOCK), 1) < HEAD_DIM
    zpad = jnp.zeros((BLOCK - N_META, BLOCK), bf16)
    outs = []
    for kh in range(ATT_KV_HEADS):
        sl = slice(kh * BLOCK, (kh + 1) * BLOCK)
        kcat = jnp.concatenate([kp_ref[0][:, sl], kc_ref[0][:, sl], kn_ref[0][:, sl],
                                km_ref[0][:, sl], zpad], axis=0)
        vcat = jnp.concatenate([vp_ref[0][:, sl], vc_ref[0][:, sl], vn_ref[0][:, sl],
                                vm_ref[0][:, sl], zpad], axis=0)
        qparts = []
        for g in range(4):
            grp = q[:, (kh * 2 + g // 2) * BLOCK:(kh * 2 + g // 2 + 1) * BLOCK]
            keep = lo if g % 2 == 0 else jnp.logical_not(lo)
            qparts.append(jnp.where(keep, grp, jnp.zeros_like(grp)))
        qg = jnp.concatenate(qparts, axis=0)
        s = _dot_nt(qg, kcat)
        s = (s + bias_ref[kh]) + pen
        sk = sink_ref[kh]
        m = jnp.maximum(jnp.max(s, axis=-1, keepdims=True), sk)
        p = jnp.exp(s - m)
        den = jnp.sum(p, axis=-1, keepdims=True) + jnp.exp(sk - m)
        o = _dot(p.astype(bf16), vcat) / den
        for c in range(2):
            outs.append(jnp.where(lo, o[(2 * c) * BLOCK:(2 * c + 1) * BLOCK],
                                  o[(2 * c + 1) * BLOCK:(2 * c + 2) * BLOCK]))
    o_ref[0] = jnp.concatenate(outs, axis=-1).astype(bf16)


def _attention(att, bias, sink):
    B, Lp, _ = att.shape
    nb = Lp // BLOCK
    prev = lambda b, j: (b, jnp.maximum(j - 1, 0), 2)
    cur = lambda b, j: (b, j, 2)
    nxt = lambda b, j: (b, jnp.minimum(j + 1, nb - 1), 2)
    prev_v = lambda b, j: (b, jnp.maximum(j - 1, 0), 3)
    cur_v = lambda b, j: (b, j, 3)
    nxt_v = lambda b, j: (b, jnp.minimum(j + 1, nb - 1), 3)
    meta_row = PAD // N_META
    return pl.pallas_call(
        _attn_kernel,
        grid=(B, nb),
        in_specs=[
            pl.BlockSpec((1, BLOCK, ATT_W), lambda b, j: (b, j, 0)),
            pl.BlockSpec((1, BLOCK, 2 * BLOCK), prev),
            pl.BlockSpec((1, BLOCK, 2 * BLOCK), cur),
            pl.BlockSpec((1, BLOCK, 2 * BLOCK), nxt),
            pl.BlockSpec((1, BLOCK, 2 * BLOCK), prev_v),
            pl.BlockSpec((1, BLOCK, 2 * BLOCK), cur_v),
            pl.BlockSpec((1, BLOCK, 2 * BLOCK), nxt_v),
            pl.BlockSpec((1, N_META, 2 * BLOCK), lambda b, j: (b, meta_row, 2)),
            pl.BlockSpec((1, N_META, 2 * BLOCK), lambda b, j: (b, meta_row, 3)),
            pl.BlockSpec((ATT_KV_HEADS, 4 * BLOCK, 4 * BLOCK), lambda b, j: (0, 0, 0)),
            pl.BlockSpec((ATT_KV_HEADS, 4 * BLOCK, 1), lambda b, j: (0, 0, 0)),
        ],
        out_specs=pl.BlockSpec((1, BLOCK, ATT_W), lambda b, j: (b, j, 0)),
        out_shape=jax.ShapeDtypeStruct((B, Lp, ATT_W), bf16),
        compiler_params=pltpu.CompilerParams(
            dimension_semantics=("parallel", "parallel"), vmem_limit_bytes=VMEM_LIMIT),
        name="attn",
    )(att, att, att, att, att, att, att, att, att, bias, sink)


def _attention_bias():
    i = jnp.arange(BLOCK)[:, None]
    c = jnp.arange(4 * BLOCK)[None, :]
    dist = jnp.abs(i + BLOCK - c)
    band = (c < 3 * BLOCK) & (dist <= BLOCK)
    meta = (c >= 3 * BLOCK) & (c < 3 * BLOCK + N_META)
    slopes = jnp.exp2(-8.0 * jnp.arange(1, ATT_HEADS + 1, dtype=f32) / ATT_HEADS)
    b = jnp.where(band[None], -(slopes[:, None, None] * dist[None].astype(f32)),
                  jnp.where(meta[None], 0.0, NEG))
    return b.reshape(ATT_KV_HEADS, 4 * BLOCK, 4 * BLOCK).astype(f32)


def _conv3_silu(x, prev8, nxt8, w3):
    rows = lax.broadcasted_iota(jnp.int32, (BLOCK, 1), 0)
    up = jnp.where(rows == 0, prev8[7:8, :], pltpu.roll(x, 1, 0))
    dn = jnp.where(rows == BLOCK - 1, nxt8[0:1, :], pltpu.roll(x, BLOCK - 1, 0))
    y = (up * w3[0:1, :] + x * w3[1:2, :]) + dn * w3[2:3, :]
    return y * _sigmoid(y)


def _gate_cumsums(gates, gbias):
    lane = lax.broadcasted_iota(jnp.int32, (1, BLOCK), 1)
    g = gates + gbias
    is_f = ((lane >= 4) & (lane < 8)) | ((lane >= 12) & (lane < 16))
    P = jnp.where(is_f, _log_sigmoid(g), jnp.where(lane < 16, g, 0.0))
    PT = P.T
    r = lax.broadcasted_iota(jnp.int32, (BLOCK, BLOCK), 0)
    c = lax.broadcasted_iota(jnp.int32, (BLOCK, BLOCK), 1)
    tril = jnp.where(r >= c, 1.0, 0.0).astype(bf16)
    triu = jnp.where(r <= c, 1.0, 0.0).astype(bf16)
    pre_col = _dot01_left(tril, P)
    suf_col = _dot01_left(triu, P)
    pre_row = _dot01_right(PT, triu)
    suf_row = _dot01_right(PT, tril)
    return P, PT, pre_col, suf_col, pre_row, suf_row


def _mlstm_state_step(P, cum_col, total_row, li_lane, lf_lane, mk, mv_ext, S_ref, N_ref, m_ref, bd):
    e_cols, a_vals, c_vals = [], [], []
    for h in range(4):
        ccol = cum_col[:, lf_lane + h:lf_lane + h + 1]
        blast = cum_col[total_row:total_row + 1, lf_lane + h:lf_lane + h + 1]
        licol = P[:, li_lane + h:li_lane + h + 1]
        wend = (blast - ccol) + licol
        m_loc = jnp.max(wend, axis=0, keepdims=True)
        e_cols.append(jnp.exp(wend - m_loc))
        m_old = m_ref[0:1, h:h + 1]
        m_new = jnp.maximum(blast + m_old, m_loc)
        a_vals.append(jnp.exp((blast + m_old) - m_new))
        c_vals.append(jnp.exp(m_loc - m_new))
        m_ref[0:1, h:h + 1] = m_new
    ke = (mk * _bcast_heads(e_cols)).T.astype(bf16)
    upd = _dot(ke, mv_ext)
    a_col = _head_rows(a_vals)
    c_col = _head_rows(c_vals)
    S_ref[...] = a_col * S_ref[...] + c_col * jnp.where(bd, upd[:, 0:HW], 0.0)
    N_ref[...] = a_col * N_ref[...] + c_col * upd[:, HW:HW + BLOCK]


def _mlstm_inputs(rest_ref, prev_ref, next_ref, cw_ref, first, last):
    x = rest_ref[0][:, 4 * HW:6 * HW]
    prev8 = jnp.where(first, 0.0, prev_ref[0])
    nxt8 = jnp.where(last, 0.0, next_ref[0])
    y = _conv3_silu(x, prev8, nxt8, cw_ref[...])
    rows = lax.broadcasted_iota(jnp.int32, (BLOCK, 1), 0)
    keep = jnp.logical_or(jnp.logical_not(first), rows >= PAD)
    mq = y[:, 0:HW]
    mk = jnp.where(keep, y[:, HW:2 * HW], 0.0) * (HEAD_DIM ** -0.5)
    return mq, mk


def _ones_ext(v_b):
    return jnp.concatenate([v_b, jnp.ones((BLOCK, BLOCK), bf16)], axis=-1)


def _decay_tables(lg_ref, deck_ref, decq_ref, flipped):
    idx = lax.broadcasted_iota(jnp.int32, (BLOCK, 1), 0).astype(f32)
    if flipped:
        idx = (BLOCK - 1.0) - idx
    lg = lg_ref[...]
    deck_ref[...] = jnp.exp(lg * ((BLOCK - 1.0) - idx))
    decq_ref[...] = jnp.exp(lg * (idx + 1.0))


def _bwd_state_kernel(rest_ref, prev_ref, next_ref, gate_ref, cw_ref, gb_ref, lgb_ref,
                      r_out, s_out, n_out, m_out,
                      R_ref, S_ref, N_ref, m_ref, deck_ref, decq_ref):
    i = pl.program_id(1)
    nc = pl.num_programs(1)
    n = nc - 1 - i

    @pl.when(i == 0)
    def _():
        R_ref[...] = jnp.zeros_like(R_ref)
        S_ref[...] = jnp.zeros_like(S_ref)
        N_ref[...] = jnp.zeros_like(N_ref)
        m_ref[...] = jnp.zeros_like(m_ref)
        _decay_tables(lgb_ref, deck_ref, decq_ref, True)

    r_out[0, 0] = R_ref[...].astype(bf16)
    s_out[0, 0] = S_ref[...].astype(bf16)
    n_out[0, 0] = N_ref[...].astype(bf16)
    m_out[0, 0] = m_ref[...]

    bd = _bd_mask()
    rest = rest_ref[0]
    k = rest[:, HW:2 * HW] * (HEAD_DIM ** -0.5)
    v_b = rest[:, 2 * HW:3 * HW].astype(bf16)
    kd = (k * deck_ref[...]).T.astype(bf16)
    g_blk = jnp.exp(lgb_ref[...] * float(BLOCK))
    R_ref[...] = R_ref[...] * g_blk + jnp.where(bd, _dot(kd, v_b), 0.0)
    _, mk = _mlstm_inputs(rest_ref, prev_ref, next_ref, cw_ref, n == 0, n == nc - 1)
    mv_ext = _ones_ext(rest[:, 6 * HW:7 * HW].astype(bf16))
    P, _, _, suf_col, _, _ = _gate_cumsums(gate_ref[0], gb_ref[...])
    _mlstm_state_step(P, suf_col, 0, 8, 12, mk, mv_ext, S_ref, N_ref, m_ref, bd)


def _mlstm_direction(P, PT, cum_col, cum_row, li_lane, lf_lane, mask, s_all, m_prev_ref):
    ws, a_int, den_i, floor = [], [], [], []
    for h in range(4):
        ccol = cum_col[:, lf_lane + h:lf_lane + h + 1]
        brow = cum_row[lf_lane + h:lf_lane + h + 1, :]
        lirow = PT[li_lane + h:li_lane + h + 1, :]
        dl = jnp.where(mask, (ccol - brow) + lirow, NEG)
        m_intra = jnp.max(dl, axis=-1, keepdims=True)
        inter = ccol + m_prev_ref[0:1, h:h + 1]
        m_t = jnp.maximum(m_intra, inter)
        a_int.append(jnp.exp(inter - m_t))
        w = jnp.exp(dl - m_t) * s_all[:, h * BLOCK:(h + 1) * BLOCK]
        den_i.append(jnp.sum(w, axis=-1, keepdims=True))
        floor.append(jnp.exp(-m_t))
        ws.append(w)
    return jnp.concatenate(ws, axis=-1), a_int, den_i, floor


def _fwd_kernel(rest_ref, prev_ref, next_ref, gate_ref, rb_ref, sb_ref, nb_ref, mb_ref,
                cw_ref, gb_ref, lgf_ref, lgb_ref, lgf4_ref, lgb4_ref, rnw_ref, mnw_ref,
                o_ref,
                R_ref, S_ref, N_ref, m_ref, deckf_ref, decqf_ref, deckb_ref, decqb_ref, dcomb_ref):
    n = pl.program_id(1)
    nc = pl.num_programs(1)

    @pl.when(n == 0)
    def _():
        R_ref[...] = jnp.zeros_like(R_ref)
        S_ref[...] = jnp.zeros_like(S_ref)
        N_ref[...] = jnp.zeros_like(N_ref)
        m_ref[...] = jnp.zeros_like(m_ref)
        _decay_tables(lgf_ref, deckf_ref, decqf_ref, False)
        _decay_tables(lgb_ref, deckb_ref, decqb_ref, True)
        t = lax.broadcasted_iota(jnp.int32, (BLOCK, 4 * BLOCK), 0)
        s = lax.broadcasted_iota(jnp.int32, (BLOCK, 4 * BLOCK), 1) & (BLOCK - 1)
        d = (t - s).astype(f32)
        dcomb_ref[...] = (jnp.where(d >= 0, jnp.exp(lgf4_ref[...] * jnp.maximum(d, 0.0)), 0.0)
                          + jnp.where(d <= 0, jnp.exp(lgb4_ref[...] * jnp.maximum(-d, 0.0)), 0.0))

    bd = _bd_mask()
    rest = rest_ref[0]

    q = rest[:, 0:HW]
    k = rest[:, HW:2 * HW] * (HEAD_DIM ** -0.5)
    v_b = rest[:, 2 * HW:3 * HW].astype(bf16)
    q_b = q.astype(bf16)
    s_all = _dot_nt(q_b, _block_stack(k.astype(bf16)))
    a = (s_all * dcomb_ref[...]).astype(bf16)
    ro = _dot(a, _block_stack(v_b))
    ro = ro + _dot((q * decqf_ref[...]).astype(bf16), R_ref[...].astype(bf16))
    ro = ro + _dot((q * decqb_ref[...]).astype(bf16), rb_ref[0, 0])
    kd = (k * deckf_ref[...]).T.astype(bf16)
    g_blk = jnp.exp(lgf_ref[...] * float(BLOCK))
    R_ref[...] = R_ref[...] * g_blk + jnp.where(bd, _dot(kd, v_b), 0.0)
    rg = rest[:, 3 * HW:4 * HW]
    ret = (rg * _sigmoid(rg)) * _head_layernorm(ro, rnw_ref[...])

    mq, mk = _mlstm_inputs(rest_ref, prev_ref, next_ref, cw_ref, n == 0, n == nc - 1)
    mq_b = mq.astype(bf16)
    mv_b = rest[:, 6 * HW:7 * HW].astype(bf16)
    ms_all = _dot_nt(mq_b, _block_stack(mk.astype(bf16)))
    P, PT, pre_col, suf_col, pre_row, suf_row = _gate_cumsums(gate_ref[0], gb_ref[...])
    tt = lax.broadcasted_iota(jnp.int32, (BLOCK, BLOCK), 0)
    ss = lax.broadcasted_iota(jnp.int32, (BLOCK, BLOCK), 1)
    v_stack = _block_stack(mv_b)
    lane128 = lax.broadcasted_iota(jnp.int32, (HW, BLOCK), 1)
    row_head = lax.broadcasted_iota(jnp.int32, (HW, BLOCK), 0) >> 6
    n_sel = lane128 == row_head

    def direction(cum_col, cum_row, li_lane, lf_lane, mask, m_prev_ref, S_b, N_b):
        w, a_int, den_i, floor = _mlstm_direction(P, PT, cum_col, cum_row, li_lane, lf_lane, mask,
                                                  ms_all, m_prev_ref)
        num = _dot(w.astype(bf16), v_stack) + _bcast_heads(a_int) * _dot(mq_b, S_b)
        qn = _dot(mq_b, jnp.where(n_sel, N_b, jnp.zeros_like(N_b)))
        dens = []
        for h in range(4):
            den = den_i[h] + a_int[h] * qn[:, h:h + 1]
            dens.append(jnp.maximum(jnp.abs(den), floor[h]))
        return num / _bcast_heads(dens)

    hf = direction(pre_col, pre_row, 0, 4, tt >= ss, m_ref, S_ref[...].astype(bf16),
                   N_ref[...].astype(bf16))
    hb = direction(suf_col, suf_row, 8, 12, tt <= ss, mb_ref[0, 0], sb_ref[0, 0], nb_ref[0, 0])
    mo = rest[:, 7 * HW:8 * HW]
    ml = _sigmoid(mo) * _head_layernorm(hf + hb, mnw_ref[...])

    _mlstm_state_step(P, pre_col, BLOCK - 1, 0, 4, mk, _ones_ext(mv_b), S_ref, N_ref, m_ref, bd)

    o_ref[0] = jnp.concatenate([ret, ml], axis=-1).astype(bf16)


def _halo_specs(nc):
    prev = lambda b, n: (b, jnp.maximum(n * 16 - 1, 0), 2)
    nxt = lambda b, n: (b, jnp.minimum((n + 1) * 16, nc * 16 - 1), 2)
    return prev, nxt


def _bwd_states(rest, gates, cw, gb, lgb):
    B, Lp, _ = rest.shape
    nc = Lp // BLOCK
    rev = lambda f: (lambda b, i: f(b, nc - 1 - i))
    prev, nxt = _halo_specs(nc)
    const2 = lambda b, i: (0, 0)
    state_spec = lambda r, c: pl.BlockSpec((1, 1, r, c), lambda b, i: (b, nc - 1 - i, 0, 0))
    return pl.pallas_call(
        _bwd_state_kernel,
        grid=(B, nc),
        in_specs=[
            pl.BlockSpec((1, BLOCK, REST_W), rev(lambda b, n: (b, n, 0))),
            pl.BlockSpec((1, 8, 2 * HW), rev(prev)),
            pl.BlockSpec((1, 8, 2 * HW), rev(nxt)),
            pl.BlockSpec((1, BLOCK, BLOCK), rev(lambda b, n: (b, n, 0))),
            pl.BlockSpec((3, 2 * HW), const2),
            pl.BlockSpec((1, BLOCK), const2),
            pl.BlockSpec((1, HW), const2),
        ],
        out_specs=[state_spec(HW, HW), state_spec(HW, HW), state_spec(HW, BLOCK), state_spec(8, BLOCK)],
        out_shape=[
            jax.ShapeDtypeStruct((B, nc, HW, HW), bf16),
            jax.ShapeDtypeStruct((B, nc, HW, HW), bf16),
            jax.ShapeDtypeStruct((B, nc, HW, BLOCK), bf16),
            jax.ShapeDtypeStruct((B, nc, 8, BLOCK), f32),
        ],
        scratch_shapes=[
            pltpu.VMEM((HW, HW), f32), pltpu.VMEM((HW, HW), f32), pltpu.VMEM((HW, BLOCK), f32),
            pltpu.VMEM((8, BLOCK), f32), pltpu.VMEM((BLOCK, HW), f32), pltpu.VMEM((BLOCK, HW), f32),
        ],
        compiler_params=pltpu.CompilerParams(
            dimension_semantics=("parallel", "arbitrary"), vmem_limit_bytes=VMEM_LIMIT),
        name="bwd_states",
    )(rest, rest, rest, gates, cw, gb, lgb)


def _fwd_sweep(rest, gates, rb, sb, nbw, mb, cw, gb, lgf, lgb, lgf4, lgb4, rnw, mnw):
    B, Lp, _ = rest.shape
    nc = Lp // BLOCK
    prev, nxt = _halo_specs(nc)
    const2 = lambda b, n: (0, 0)
    state_spec = lambda r, c: pl.BlockSpec((1, 1, r, c), lambda b, n: (b, n, 0, 0))
    return pl.pallas_call(
        _fwd_kernel,
        grid=(B, nc),
        in_specs=[
            pl.BlockSpec((1, BLOCK, REST_W), lambda b, n: (b, n, 0)),
            pl.BlockSpec((1, 8, 2 * HW), prev),
            pl.BlockSpec((1, 8, 2 * HW), nxt),
            pl.BlockSpec((1, BLOCK, BLOCK), lambda b, n: (b, n, 0)),
            state_spec(HW, HW), state_spec(HW, HW), state_spec(HW, BLOCK), state_spec(8, BLOCK),
            pl.BlockSpec((3, 2 * HW), const2),
            pl.BlockSpec((1, BLOCK), const2),
            pl.BlockSpec((1, HW), const2),
            pl.BlockSpec((1, HW), const2),
            pl.BlockSpec((1, 4 * BLOCK), const2),
            pl.BlockSpec((1, 4 * BLOCK), const2),
            pl.BlockSpec((1, HW), const2),
            pl.BlockSpec((1, HW), const2),
        ],
        out_specs=pl.BlockSpec((1, BLOCK, 2 * HW), lambda b, n: (b, n, 0)),
        out_shape=jax.ShapeDtypeStruct((B, Lp, 2 * HW), bf16),
        scratch_shapes=[
            pltpu.VMEM((HW, HW), f32), pltpu.VMEM((HW, HW), f32), pltpu.VMEM((HW, BLOCK), f32),
            pltpu.VMEM((8, BLOCK), f32),
            pltpu.VMEM((BLOCK, HW), f32), pltpu.VMEM((BLOCK, HW), f32),
            pltpu.VMEM((BLOCK, HW), f32), pltpu.VMEM((BLOCK, HW), f32),
            pltpu.VMEM((BLOCK, 4 * BLOCK), f32),
        ],
        compiler_params=pltpu.CompilerParams(
            dimension_semantics=("parallel", "arbitrary"), vmem_limit_bytes=VMEM_LIMIT),
        name="fwd_sweep",
    )(rest, rest, rest, gates, rb, sb, nbw, mb, cw, gb, lgf, lgb, lgf4, lgb4, rnw, mnw)


def _out_kernel(h_ref, a_ref, rm_ref, w_ref, o_ref):
    y = h_ref[0] + (_dot(a_ref[0], w_ref[0:ATT_W, :]) + _dot(rm_ref[0], w_ref[ATT_W:, :]))
    rows = lax.broadcasted_iota(jnp.int32, (y.shape[0], 1), 0)
    keep = jnp.logical_or(pl.program_id(1) > 0, rows >= PAD)
    o_ref[0] = jnp.where(keep, y, 0.0)


def _out_proj(h, att_o, rm, w_out, tm):
    B, Lp, _ = h.shape
    return pl.pallas_call(
        _out_kernel,
        grid=(B, Lp // tm),
        in_specs=[
            pl.BlockSpec((1, tm, D_MODEL), lambda b, i: (b, i, 0)),
            pl.BlockSpec((1, tm, ATT_W), lambda b, i: (b, i, 0)),
            pl.BlockSpec((1, tm, 2 * HW), lambda b, i: (b, i, 0)),
            pl.BlockSpec((D_MODEL, D_MODEL), lambda b, i: (0, 0)),
        ],
        out_specs=pl.BlockSpec((1, tm, D_MODEL), lambda b, i: (b, i, 0)),
        out_shape=jax.ShapeDtypeStruct(h.shape, f32),
        compiler_params=pltpu.CompilerParams(
            dimension_semantics=("parallel", "parallel"), vmem_limit_bytes=VMEM_LIMIT),
        name="out_proj",
    )(h, att_o, rm, w_out)


def _ffn_kernel(h_ref, hp_ref, hn_ref, n2_ref, wup_ref, cw_ref, wdn_ref, o_ref, acc_ref):
    i = pl.program_id(1)
    last = i == pl.num_programs(1) - 1
    nw = n2_ref[...]

    def norm(x):
        ms = jnp.mean(x * x, axis=-1, keepdims=True)
        return ((x * lax.rsqrt(ms + EPS)) * nw).astype(bf16)

    x = h_ref[0]
    tm = x.shape[0]
    xn = norm(x)
    halo = norm(jnp.concatenate([hp_ref[0], jnp.where(last, 0.0, hn_ref[0])], axis=0))
    rows = lax.broadcasted_iota(jnp.int32, (tm, 1), 0)
    first_row = rows == 0
    last_row = rows == tm - 1

    def conv(u, uh, w3):
        up = jnp.where(first_row, uh[7:8, :], pltpu.roll(u, 1, 0))
        dn = jnp.where(last_row, uh[8:9, :], pltpu.roll(u, tm - 1, 0))
        return (up * w3[0:1, :] + u * w3[1:2, :]) + dn * w3[2:3, :]

    for c in range(D_FF // FF_CHUNK):
        gs = slice(c * FF_CHUNK, (c + 1) * FF_CHUNK)
        vs = slice(D_FF + c * FF_CHUNK, D_FF + (c + 1) * FF_CHUNK)
        g = conv(_dot(xn, wup_ref[:, gs]), _dot(halo, wup_ref[:, gs]), cw_ref[:, gs])
        v = conv(_dot(xn, wup_ref[:, vs]), _dot(halo, wup_ref[:, vs]), cw_ref[:, vs])
        act = ((g * _sigmoid(g)) * v).astype(bf16)
        part = _dot(act, wdn_ref[gs, :])
        if c == 0:
            acc_ref[...] = part
        else:
            acc_ref[...] += part
    y = x + acc_ref[...]
    keep = jnp.logical_or(i > 0, rows >= PAD)
    o_ref[0] = jnp.where(keep, y, 0.0)


def _ffn(h, n2, w_up, cw, w_dn, tm):
    B, Lp, _ = h.shape
    nt = Lp // tm
    r8 = tm // 8
    return pl.pallas_call(
        _ffn_kernel,
        grid=(B, nt),
        in_specs=[
            pl.BlockSpec((1, tm, D_MODEL), lambda b, i: (b, i, 0)),
            pl.BlockSpec((1, 8, D_MODEL), lambda b, i: (b, jnp.maximum(i * r8 - 1, 0), 0)),
            pl.BlockSpec((1, 8, D_MODEL), lambda b, i: (b, jnp.minimum((i + 1) * r8, nt * r8 - 1), 0)),
            pl.BlockSpec((1, D_MODEL), lambda b, i: (0, 0)),
            pl.BlockSpec((D_MODEL, 2 * D_FF), lambda b, i: (0, 0), pipeline_mode=pl.Buffered(1)),
            pl.BlockSpec((3, 2 * D_FF), lambda b, i: (0, 0)),
            pl.BlockSpec((D_FF, D_MODEL), lambda b, i: (0, 0), pipeline_mode=pl.Buffered(1)),
        ],
        out_specs=pl.BlockSpec((1, tm, D_MODEL), lambda b, i: (b, i, 0)),
        out_shape=jax.ShapeDtypeStruct(h.shape, f32),
        scratch_shapes=[pltpu.VMEM((tm, D_MODEL), f32)],
        compiler_params=pltpu.CompilerParams(
            dimension_semantics=("parallel", "parallel"), vmem_limit_bytes=VMEM_LIMIT),
        name="ffn",
    )(h, h, h, n2, w_up, cw, w_dn)


def _lane_groups(x, width, reps):
    del reps
    return jnp.repeat(x, width, axis=-1)[:, None, :]


def _prepare_params(norm1_w, w_in, attn_q_norm_w, attn_k_norm_w, attn_sink, ret_decay_logit,
                    ret_norm_w, mlstm_conv_w, mlstm_gate_b, mlstm_norm_w, w_out, norm2_w,
                    ffn_up, ffn_conv_w, ffn_down):
    scale = HEAD_DIM ** -0.5
    qrow = jnp.tile(attn_q_norm_w.astype(f32) * scale, (1, 2))
    krow = jnp.tile(attn_k_norm_w.astype(f32), (1, 2))
    qkw = jnp.concatenate([jnp.stack([qrow] * 4 + [krow], axis=1),
                           jnp.zeros((DEPTH, 3, BLOCK), f32)], axis=1)
    sink = jnp.repeat(attn_sink.astype(f32).reshape(DEPTH, ATT_KV_HEADS, 4), BLOCK, axis=-1)
    log_gamma = jax.nn.log_sigmoid(ret_decay_logit.astype(f32))
    gb = jnp.concatenate([mlstm_gate_b.astype(f32), jnp.zeros((DEPTH, BLOCK - N_GATES), f32)], axis=-1)
    return dict(
        n1=norm1_w.astype(f32)[:, None, :],
        w_main=w_in[:, :, :MAIN_W].astype(bf16),
        w_gate=jnp.concatenate([w_in[:, :, MAIN_W:], jnp.zeros((DEPTH, D_MODEL, BLOCK - N_GATES), w_in.dtype)],
                               axis=-1).astype(bf16),
        qkw=qkw,
        sink=sink[..., None],
        lgf=_lane_groups(log_gamma[:, 0], HEAD_DIM, 4),
        lgb=_lane_groups(log_gamma[:, 1], HEAD_DIM, 4),
        lgf4=_lane_groups(log_gamma[:, 0], BLOCK, 4),
        lgb4=_lane_groups(log_gamma[:, 1], BLOCK, 4),
        rnw=ret_norm_w.astype(f32)[:, None, :],
        cw=mlstm_conv_w.astype(f32),
        gb=gb[:, None, :],
        mnw=mlstm_norm_w.astype(f32)[:, None, :],
        w_out=w_out.astype(bf16),
        n2=norm2_w.astype(f32)[:, None, :],
        w_up=ffn_up.astype(bf16),
        fcw=ffn_conv_w.astype(f32),
        w_dn=ffn_down.astype(bf16),
    )


def _trunk(x, meta_tokens, params, bias):
    B, seq, _ = x.shape
    lp = seq + BLOCK
    tm = _row_tile(lp)
    meta = jnp.broadcast_to(meta_tokens[None].astype(x.dtype), (B, N_META, D_MODEL))
    h = jnp.concatenate([jnp.zeros((B, PAD, D_MODEL), x.dtype), meta, x], axis=1)

    def layer(h, p):
        att, rest, gates = _project(h, p["n1"], p["w_main"], p["w_gate"], p["qkw"], tm)
        att_o = _attention(att, bias, p["sink"])
        rb, sb, nbw, mb = _bwd_states(rest, gates, p["cw"], p["gb"], p["lgb"])
        rm = _fwd_sweep(rest, gates, rb, sb, nbw, mb, p["cw"], p["gb"], p["lgf"], p["lgb"],
                        p["lgf4"], p["lgb4"], p["rnw"], p["mnw"])
        h = _out_proj(h, att_o, rm, p["w_out"], tm)
        h = _ffn(h, p["n2"], p["w_up"], p["fcw"], p["w_dn"], tm)
        return h, None

    h, _ = lax.scan(layer, h, params)
    return h[:, BLOCK:]


def kernel(x_prompt, x_sample, meta_tokens, norm1_w, w_in, attn_q_norm_w, attn_k_norm_w, attn_sink,
           ret_decay_logit, ret_norm_w, mlstm_conv_w, mlstm_gate_b, mlstm_norm_w, w_out, norm2_w,
           ffn_up, ffn_conv_w, ffn_down):
    params = _prepare_params(norm1_w, w_in, attn_q_norm_w, attn_k_norm_w, attn_sink, ret_decay_logit,
                             ret_norm_w, mlstm_conv_w, mlstm_gate_b, mlstm_norm_w, w_out, norm2_w,
                             ffn_up, ffn_conv_w, ffn_down)
    bias = _attention_bias()
    y_prompt = _trunk(x_prompt, meta_tokens, params, bias)
    y_sample = _trunk(x_sample, meta_tokens, params, bias)
    return (y_prompt, y_sample)
```

```python
import functools
import math

import jax
import jax.numpy as jnp
from jax import lax
from jax.experimental import pallas as pl
from jax.experimental.pallas import tpu as pltpu

f32 = jnp.float32
bf16 = jnp.bfloat16

D_MODEL = 1024
DEPTH = 4
N_META = 16
BLOCK = 128
PAD = BLOCK - N_META
HEAD_DIM = 64
ATT_HEADS = 8
ATT_KV_HEADS = 2
ATT_W = 512
KV_W = 128
HW = 256
N_GATES = 16
MAIN_W = 2816
REST_W = 2048
ATT_OUT_W = 1024
D_FF = 2816
FF_CHUNK = 256
DOWN_GROUP = 4
EPS = 1e-6
NEG = -1e30
VMEM_LIMIT = 56 * 1024 * 1024


def _row_tile(lp):
    nb = lp // BLOCK
    best = 1
    for t in range(1, 7):
        if nb % t == 0:
            best = t
    return best * BLOCK


def _sigmoid(x):
    return 1.0 / (1.0 + jnp.exp(-x))


def _log_sigmoid(x):
    return jnp.minimum(x, 0.0) - jnp.log(1.0 + jnp.exp(-jnp.abs(x)))


def _seg64_mean(x, lo):
    a = jnp.sum(jnp.where(lo, x, 0.0), axis=-1, keepdims=True)
    b = jnp.sum(jnp.where(lo, 0.0, x), axis=-1, keepdims=True)
    return jnp.where(lo, a, b) * (1.0 / HEAD_DIM)


def _head_layernorm(x, w):
    lo = lax.broadcasted_iota(jnp.int32, (1, BLOCK), 1) < HEAD_DIM
    outs = []
    for c in range(x.shape[1] // BLOCK):
        xc = x[:, c * BLOCK:(c + 1) * BLOCK]
        mu = _seg64_mean(xc, lo)
        d = xc - mu
        var = _seg64_mean(d * d, lo)
        outs.append(d * lax.rsqrt(var + EPS))
    return jnp.concatenate(outs, axis=-1) * w


def _bcast_heads(cols):
    lane = lax.broadcasted_iota(jnp.int32, (1, HW), 1)
    return jnp.where(lane < 64, cols[0],
                     jnp.where(lane < 128, cols[1], jnp.where(lane < 192, cols[2], cols[3])))


def _head_rows(vals):
    row = lax.broadcasted_iota(jnp.int32, (HW, 1), 0)
    return jnp.where(row < 64, vals[0],
                     jnp.where(row < 128, vals[1], jnp.where(row < 192, vals[2], vals[3])))


def _block_stack(x_b):
    lane = lax.broadcasted_iota(jnp.int32, (1, HW), 1)
    parts = []
    for h in range(4):
        keep = (lane >= h * HEAD_DIM) & (lane < (h + 1) * HEAD_DIM)
        parts.append(jnp.where(keep, x_b, jnp.zeros_like(x_b)))
    return jnp.concatenate(parts, axis=0)


def _bd_mask():
    r = lax.broadcasted_iota(jnp.int32, (HW, HW), 0) >> 6
    c = lax.broadcasted_iota(jnp.int32, (HW, HW), 1) >> 6
    return r == c


def _dot(a, b):
    return jnp.dot(a, b, preferred_element_type=f32)


def _dot_nt(a, b):
    return lax.dot_general(a, b, (((1,), (1,)), ((), ())), preferred_element_type=f32)


def _split3(x):
    p1 = x.astype(bf16)
    r1 = x - p1.astype(f32)
    p2 = r1.astype(bf16)
    p3 = (r1 - p2.astype(f32)).astype(bf16)
    return p1, p2, p3


def _dot01_left(m01, x):
    p1, p2, p3 = _split3(x)
    return (_dot(m01, p3) + _dot(m01, p2)) + _dot(m01, p1)


def _dot01_right(x, m01):
    p1, p2, p3 = _split3(x)
    return (_dot(p3, m01) + _dot(p2, m01)) + _dot(p1, m01)


def _proj_kernel(h_ref, n1_ref, w_ref, wg_ref, qkw_ref, att_ref, rest_ref, gate_ref):
    x = h_ref[0]
    ms = jnp.mean(x * x, axis=-1, keepdims=True)
    hn = ((x * lax.rsqrt(ms + EPS)) * n1_ref[...]).astype(bf16)
    pa = _dot(hn, w_ref[:, 0:ATT_W + 2 * KV_W])
    lo = lax.broadcasted_iota(jnp.int32, (1, BLOCK), 1) < HEAD_DIM
    pieces = []
    for c in range(5):
        xc = pa[:, c * BLOCK:(c + 1) * BLOCK]
        msq = _seg64_mean(xc * xc, lo)
        pieces.append((xc * lax.rsqrt(msq + EPS)) * qkw_ref[c:c + 1, :])
    k = pieces[4]
    v = pa[:, ATT_W + KV_W:ATT_W + 2 * KV_W]
    k_sw = pltpu.roll(k, HEAD_DIM, 1)
    v_sw = pltpu.roll(v, HEAD_DIM, 1)
    out = pieces[:4] + [jnp.where(lo, k, k_sw), jnp.where(lo, k_sw, k),
                        jnp.where(lo, v, v_sw), jnp.where(lo, v_sw, v)]
    att_ref[0] = jnp.concatenate(out, axis=-1).astype(bf16)
    rest_ref[0] = _dot(hn, w_ref[:, ATT_W + 2 * KV_W:MAIN_W])
    gate_ref[0] = _dot(hn, wg_ref[...])


def _project(h, n1, w_main, w_gate, qkw, tm):
    B, Lp, _ = h.shape
    grid = (B, Lp // tm)
    return pl.pallas_call(
        _proj_kernel,
        grid=grid,
        in_specs=[
            pl.BlockSpec((1, tm, D_MODEL), lambda b, i: (b, i, 0)),
            pl.BlockSpec((1, D_MODEL), lambda b, i: (0, 0)),
            pl.BlockSpec((D_MODEL, MAIN_W), lambda b, i: (0, 0)),
            pl.BlockSpec((D_MODEL, BLOCK), lambda b, i: (0, 0)),
            pl.BlockSpec((8, BLOCK), lambda b, i: (0, 0)),
        ],
        out_specs=[
            pl.BlockSpec((1, tm, ATT_OUT_W), lambda b, i: (b, i, 0)),
            pl.BlockSpec((1, tm, REST_W), lambda b, i: (b, i, 0)),
            pl.BlockSpec((1, tm, BLOCK), lambda b, i: (b, i, 0)),
        ],
        out_shape=[
            jax.ShapeDtypeStruct((B, Lp, ATT_OUT_W), bf16),
            jax.ShapeDtypeStruct((B, Lp, REST_W), f32),
            jax.ShapeDtypeStruct((B, Lp, BLOCK), f32),
        ],
        compiler_params=pltpu.CompilerParams(
            dimension_semantics=("parallel", "parallel"), vmem_limit_bytes=VMEM_LIMIT),
        name="proj",
    )(h, n1, w_main, w_gate, qkw)


def _attn_kernel(q_ref, kvp_ref, kvc_ref, kvn_ref, kvm_ref, bias_ref, sink_ref, o_ref, *, nblk):
    i = pl.program_id(1)
    nb = pl.num_programs(1) * nblk
    lo = lax.broadcasted_iota(jnp.int32, (1, BLOCK), 1) < HEAD_DIM
    zpad = jnp.zeros((BLOCK - N_META, BLOCK), bf16)
    kvs = ([kvp_ref[0]] + [kvc_ref[0, u * BLOCK:(u + 1) * BLOCK, :] for u in range(nblk)] + [kvn_ref[0]])
    meta = kvm_ref[0]

    def scores(t, kh):
        j = i * nblk + t
        var = jnp.where(j == 0, 1, jnp.where(j == 1, 2, jnp.where(j == nb - 1, 3, 0)))
        q = q_ref[0, t * BLOCK:(t + 1) * BLOCK, :]
        ksl = slice(kh * BLOCK, (kh + 1) * BLOCK)
        kcat = jnp.concatenate([kvs[t][:, ksl], kvs[t + 1][:, ksl], kvs[t + 2][:, ksl],
                                meta[:, ksl], zpad], axis=0)
        qparts = []
        for g in range(4):
            grp = q[:, (kh * 2 + g // 2) * BLOCK:(kh * 2 + g // 2 + 1) * BLOCK]
            keep = lo if g % 2 == 0 else jnp.logical_not(lo)
            qparts.append(jnp.where(keep, grp, jnp.zeros_like(grp)))
        qg = jnp.concatenate(qparts, axis=0)
        return _dot_nt(qg, kcat) + bias_ref[var * ATT_KV_HEADS + kh]

    units = [(t, kh) for t in range(nblk) for kh in range(ATT_KV_HEADS)]
    s_next = scores(*units[0])
    outs = []
    for n, (t, kh) in enumerate(units):
        s = s_next
        if n + 1 < len(units):
            s_next = scores(*units[n + 1])
        vsl = slice((2 + kh) * BLOCK, (3 + kh) * BLOCK)
        vcat = jnp.concatenate([kvs[t][:, vsl], kvs[t + 1][:, vsl], kvs[t + 2][:, vsl],
                                meta[:, vsl], zpad], axis=0)
        sk = sink_ref[kh]
        m = jnp.maximum(jnp.max(s, axis=-1, keepdims=True), sk)
        p = jnp.exp(s - m)
        den = jnp.sum(p, axis=-1, keepdims=True) + jnp.exp(sk - m)
        o = _dot(p.astype(bf16), vcat) / den
        for c in range(2):
            outs.append(jnp.where(lo, o[(2 * c) * BLOCK:(2 * c + 1) * BLOCK],
                                  o[(2 * c + 1) * BLOCK:(2 * c + 2) * BLOCK]))
        if kh == ATT_KV_HEADS - 1:
            o_ref[0, t * BLOCK:(t + 1) * BLOCK, :] = jnp.concatenate(outs, axis=-1).astype(bf16)
            outs = []


def _attention(att, bias, sink, tm):
    B, Lp, _ = att.shape
    nblk = tm // BLOCK
    nt = Lp // tm
    assert Lp // BLOCK >= 4
    meta_row = PAD // N_META
    return pl.pallas_call(
        functools.partial(_attn_kernel, nblk=nblk),
        grid=(B, nt),
        in_specs=[
            pl.BlockSpec((1, tm, ATT_W), lambda b, i: (b, i, 0)),
            pl.BlockSpec((1, BLOCK, ATT_W), lambda b, i: (b, jnp.maximum(i * nblk - 1, 0), 1)),
            pl.BlockSpec((1, tm, ATT_W), lambda b, i: (b, i, 1)),
            pl.BlockSpec((1, BLOCK, ATT_W), lambda b, i: (b, jnp.minimum((i + 1) * nblk, nt * nblk - 1), 1)),
            pl.BlockSpec((1, N_META, ATT_W), lambda b, i: (b, meta_row, 1)),
            pl.BlockSpec((4 * ATT_KV_HEADS, 4 * BLOCK, 4 * BLOCK), lambda b, i: (0, 0, 0),
                         pipeline_mode=pl.Buffered(1)),
            pl.BlockSpec((ATT_KV_HEADS, 4 * BLOCK, 1), lambda b, i: (0, 0, 0)),
        ],
        out_specs=pl.BlockSpec((1, tm, ATT_W), lambda b, i: (b, i, 0)),
        out_shape=jax.ShapeDtypeStruct((B, Lp, ATT_W), bf16),
        compiler_params=pltpu.CompilerParams(
            dimension_semantics=("parallel", "parallel"), vmem_limit_bytes=VMEM_LIMIT),
        name="attn",
    )(att, att, att, att, att, bias, sink)


def _attention_bias():
    i = jnp.arange(BLOCK)[:, None]
    c = jnp.arange(4 * BLOCK)[None, :]
    dist = jnp.abs(i + BLOCK - c)
    band = (c < 3 * BLOCK) & (dist <= BLOCK)
    meta = (c >= 3 * BLOCK) & (c < 3 * BLOCK + N_META)
    slopes = jnp.exp2(-8.0 * jnp.arange(1, ATT_HEADS + 1, dtype=f32) / ATT_HEADS)
    blk = c // BLOCK
    variants = []
    for dead in ((), (0, 1), (0,), (2,)):
        ok = band
        for d in dead:
            ok = ok & (blk != d)
        b = jnp.where(ok[None], -(slopes[:, None, None] * dist[None].astype(f32)),
                      jnp.where(meta[None], 0.0, NEG))
        variants.append(b.reshape(ATT_KV_HEADS, 4 * BLOCK, 4 * BLOCK))
    return jnp.concatenate(variants, axis=0).astype(f32)


def _conv3_silu(x, prev8, nxt8, w3):
    tm = x.shape[0]
    rows = lax.broadcasted_iota(jnp.int32, (tm, 1), 0)
    up = jnp.where(rows == 0, prev8[7:8, :], pltpu.roll(x, 1, 0))
    dn = jnp.where(rows == tm - 1, nxt8[0:1, :], pltpu.roll(x, tm - 1, 0))
    y = (up * w3[0:1, :] + x * w3[1:2, :]) + dn * w3[2:3, :]
    return y * _sigmoid(y)


def _gate_cumsums(gates, gbias):
    lane = lax.broadcasted_iota(jnp.int32, (1, BLOCK), 1)
    g = gates + gbias
    is_f = ((lane >= 4) & (lane < 8)) | ((lane >= 12) & (lane < 16))
    P = jnp.where(is_f, _log_sigmoid(g), jnp.where(lane < 16, g, 0.0))
    PT = P.T
    r = lax.broadcasted_iota(jnp.int32, (BLOCK, BLOCK), 0)
    c = lax.broadcasted_iota(jnp.int32, (BLOCK, BLOCK), 1)
    tril = jnp.where(r >= c, 1.0, 0.0).astype(bf16)
    triu = jnp.where(r <= c, 1.0, 0.0).astype(bf16)
    pre_col = _dot01_left(tril, P)
    suf_col = _dot01_left(triu, P)
    pre_row = _dot01_right(PT, triu)
    suf_row = _dot01_right(PT, tril)
    return P, PT, pre_col, suf_col, pre_row, suf_row


def _mlstm_state_step(P, cum_col, total_row, li_lane, lf_lane, mk, mv_ext, S_ref, N_ref, m_ref, bd):
    e_cols, a_vals, c_vals = [], [], []
    for h in range(4):
        ccol = cum_col[:, lf_lane + h:lf_lane + h + 1]
        blast = cum_col[total_row:total_row + 1, lf_lane + h:lf_lane + h + 1]
        licol = P[:, li_lane + h:li_lane + h + 1]
        wend = (blast - ccol) + licol
        m_loc = jnp.max(wend, axis=0, keepdims=True)
        e_cols.append(jnp.exp(wend - m_loc))
        m_old = m_ref[0:1, h:h + 1]
        m_new = jnp.maximum(blast + m_old, m_loc)
        a_vals.append(jnp.exp((blast + m_old) - m_new))
        c_vals.append(jnp.exp(m_loc - m_new))
        m_ref[0:1, h:h + 1] = m_new
    ke = (mk * _bcast_heads(e_cols)).T.astype(bf16)
    upd = _dot(ke, mv_ext)
    a_col = _head_rows(a_vals)
    c_col = _head_rows(c_vals)
    S_ref[...] = a_col * S_ref[...] + c_col * jnp.where(bd, upd[:, 0:HW], 0.0)
    N_ref[...] = a_col * N_ref[...] + c_col * upd[:, HW:HW + BLOCK]


def _mlstm_inputs(rest_ref, prev_ref, next_ref, cw_ref, first, last):
    x = rest_ref[0, :, 4 * HW:6 * HW]
    tm = x.shape[0]
    prev8 = jnp.where(first, 0.0, prev_ref[0])
    nxt8 = jnp.where(last, 0.0, next_ref[0])
    y = _conv3_silu(x, prev8, nxt8, cw_ref[...])
    rows = lax.broadcasted_iota(jnp.int32, (tm, 1), 0)
    keep = jnp.logical_or(jnp.logical_not(first), rows >= PAD)
    mq = y[:, 0:HW]
    mk = jnp.where(keep, y[:, HW:2 * HW], 0.0) * (HEAD_DIM ** -0.5)
    return mq, mk


def _ones_ext(v_b):
    return jnp.concatenate([v_b, jnp.ones((BLOCK, BLOCK), bf16)], axis=-1)


def _decay_tables(lg_ref, deck_ref, decq_ref, flipped):
    idx = lax.broadcasted_iota(jnp.int32, (BLOCK, 1), 0).astype(f32)
    if flipped:
        idx = (BLOCK - 1.0) - idx
    lg = lg_ref[...]
    deck_ref[...] = jnp.exp(lg * ((BLOCK - 1.0) - idx))
    decq_ref[...] = jnp.exp(lg * (idx + 1.0))


def _bwd_state_kernel(rest_ref, prev_ref, next_ref, gate_ref, cw_ref, gb_ref, lgb_ref,
                      r_out, s_out, n_out, m_out,
                      R_ref, S_ref, N_ref, m_ref, deck_ref, decq_ref, *, nchunk):
    i = pl.program_id(1)
    nt = pl.num_programs(1)
    ti = nt - 1 - i

    @pl.when(i == 0)
    def _():
        R_ref[...] = jnp.zeros_like(R_ref)
        S_ref[...] = jnp.zeros_like(S_ref)
        N_ref[...] = jnp.zeros_like(N_ref)
        m_ref[...] = jnp.zeros_like(m_ref)
        _decay_tables(lgb_ref, deck_ref, decq_ref, True)

    bd = _bd_mask()
    g_blk = jnp.exp(lgb_ref[...] * float(BLOCK))
    _, mk_all = _mlstm_inputs(rest_ref, prev_ref, next_ref, cw_ref, ti == 0, ti == nt - 1)
    for c in reversed(range(nchunk)):
        rows = slice(c * BLOCK, (c + 1) * BLOCK)
        r_out[0, c] = R_ref[...].astype(bf16)
        s_out[0, c] = S_ref[...].astype(bf16)
        n_out[0, c] = N_ref[...].astype(bf16)
        m_out[0, c] = m_ref[...]
        k = rest_ref[0, rows, HW:2 * HW] * (HEAD_DIM ** -0.5)
        v_b = rest_ref[0, rows, 2 * HW:3 * HW].astype(bf16)
        kd = (k * deck_ref[...]).T.astype(bf16)
        R_ref[...] = R_ref[...] * g_blk + jnp.where(bd, _dot(kd, v_b), 0.0)
        mv_ext = _ones_ext(rest_ref[0, rows, 6 * HW:7 * HW].astype(bf16))
        P, _, _, suf_col, _, _ = _gate_cumsums(gate_ref[0, rows, :], gb_ref[...])
        _mlstm_state_step(P, suf_col, 0, 8, 12, mk_all[rows], mv_ext, S_ref, N_ref, m_ref, bd)


def _mlstm_direction(P, PT, cum_col, cum_row, li_lane, lf_lane, mask, s_all, m_prev_ref):
    ws, a_int, den_i, floor = [], [], [], []
    for h in range(4):
        ccol = cum_col[:, lf_lane + h:lf_lane + h + 1]
        brow = cum_row[lf_lane + h:lf_lane + h + 1, :]
        lirow = PT[li_lane + h:li_lane + h + 1, :]
        dl = jnp.where(mask, (ccol - brow) + lirow, NEG)
        m_intra = jnp.max(dl, axis=-1, keepdims=True)
        inter = ccol + m_prev_ref[0:1, h:h + 1]
        m_t = jnp.maximum(m_intra, inter)
        a_int.append(jnp.exp(inter - m_t))
        w = jnp.exp(dl - m_t) * s_all[:, h * BLOCK:(h + 1) * BLOCK]
        den_i.append(jnp.sum(w, axis=-1, keepdims=True))
        floor.append(jnp.exp(-m_t))
        ws.append(w)
    return jnp.concatenate(ws, axis=-1), a_int, den_i, floor


def _fwd_kernel(rest_ref, prev_ref, next_ref, gate_ref, rb_ref, sb_ref, nb_ref, mb_ref,
                cw_ref, gb_ref, lgf_ref, lgb_ref, lgf4_ref, lgb4_ref, rnw_ref, mnw_ref,
                o_ref,
                R_ref, S_ref, N_ref, m_ref, deckf_ref, decqf_ref, deckb_ref, decqb_ref, dcomb_ref,
                *, nchunk):
    i = pl.program_id(1)
    nt = pl.num_programs(1)

    @pl.when(i == 0)
    def _():
        R_ref[...] = jnp.zeros_like(R_ref)
        S_ref[...] = jnp.zeros_like(S_ref)
        N_ref[...] = jnp.zeros_like(N_ref)
        m_ref[...] = jnp.zeros_like(m_ref)
        _decay_tables(lgf_ref, deckf_ref, decqf_ref, False)
        _decay_tables(lgb_ref, deckb_ref, decqb_ref, True)
        t = lax.broadcasted_iota(jnp.int32, (BLOCK, 4 * BLOCK), 0)
        s = lax.broadcasted_iota(jnp.int32, (BLOCK, 4 * BLOCK), 1) & (BLOCK - 1)
        d = (t - s).astype(f32)
        dcomb_ref[...] = (jnp.where(d >= 0, jnp.exp(lgf4_ref[...] * jnp.maximum(d, 0.0)), 0.0)
                          + jnp.where(d <= 0, jnp.exp(lgb4_ref[...] * jnp.maximum(-d, 0.0)), 0.0))

    bd = _bd_mask()
    g_blk = jnp.exp(lgf_ref[...] * float(BLOCK))
    tt = lax.broadcasted_iota(jnp.int32, (BLOCK, BLOCK), 0)
    ss = lax.broadcasted_iota(jnp.int32, (BLOCK, BLOCK), 1)
    lane128 = lax.broadcasted_iota(jnp.int32, (HW, BLOCK), 1)
    row_head = lax.broadcasted_iota(jnp.int32, (HW, BLOCK), 0) >> 6
    n_sel = lane128 == row_head
    mq_all, mk_all = _mlstm_inputs(rest_ref, prev_ref, next_ref, cw_ref, i == 0, i == nt - 1)

    for c in range(nchunk):
        rows = slice(c * BLOCK, (c + 1) * BLOCK)

        q = rest_ref[0, rows, 0:HW]
        k = rest_ref[0, rows, HW:2 * HW] * (HEAD_DIM ** -0.5)
        v_b = rest_ref[0, rows, 2 * HW:3 * HW].astype(bf16)
        q_b = q.astype(bf16)
        s_all = _dot_nt(q_b, _block_stack(k.astype(bf16)))
        a = (s_all * dcomb_ref[...]).astype(bf16)
        ro = _dot(a, _block_stack(v_b))
        ro = ro + _dot((q * decqf_ref[...]).astype(bf16), R_ref[...].astype(bf16))
        ro = ro + _dot((q * decqb_ref[...]).astype(bf16), rb_ref[0, c])
        kd = (k * deckf_ref[...]).T.astype(bf16)
        R_ref[...] = R_ref[...] * g_blk + jnp.where(bd, _dot(kd, v_b), 0.0)
        rg = rest_ref[0, rows, 3 * HW:4 * HW]
        ret = (rg * _sigmoid(rg)) * _head_layernorm(ro, rnw_ref[...])

        mq_b = mq_all[rows].astype(bf16)
        mk = mk_all[rows]
        mv_b = rest_ref[0, rows, 6 * HW:7 * HW].astype(bf16)
        ms_all = _dot_nt(mq_b, _block_stack(mk.astype(bf16)))
        P, PT, pre_col, suf_col, pre_row, suf_row = _gate_cumsums(gate_ref[0, rows, :], gb_ref[...])
        v_stack = _block_stack(mv_b)

        def direction(cum_col, cum_row, li_lane, lf_lane, mask, m_prev_ref, S_b, N_b):
            w, a_int, den_i, floor = _mlstm_direction(P, PT, cum_col, cum_row, li_lane, lf_lane, mask,
                                                      ms_all, m_prev_ref)
            num = _dot(w.astype(bf16), v_stack) + _bcast_heads(a_int) * _dot(mq_b, S_b)
            qn = _dot(mq_b, jnp.where(n_sel, N_b, jnp.zeros_like(N_b)))
            dens = []
            for h in range(4):
                den = den_i[h] + a_int[h] * qn[:, h:h + 1]
                dens.append(jnp.maximum(jnp.abs(den), floor[h]))
            return num / _bcast_heads(dens)

        hf = direction(pre_col, pre_row, 0, 4, tt >= ss, m_ref, S_ref[...].astype(bf16),
                       N_ref[...].astype(bf16))
        hb = direction(suf_col, suf_row, 8, 12, tt <= ss, mb_ref[0, c], sb_ref[0, c], nb_ref[0, c])
        mo = rest_ref[0, rows, 7 * HW:8 * HW]
        ml = _sigmoid(mo) * _head_layernorm(hf + hb, mnw_ref[...])

        _mlstm_state_step(P, pre_col, BLOCK - 1, 0, 4, mk, _ones_ext(mv_b), S_ref, N_ref, m_ref, bd)

        o_ref[0, rows, :] = jnp.concatenate([ret, ml], axis=-1).astype(bf16)


def _halo_specs(tm, nt):
    r8 = tm // 8
    prev = lambda b, t: (b, jnp.maximum(t * r8 - 1, 0), 2)
    nxt = lambda b, t: (b, jnp.minimum((t + 1) * r8, nt * r8 - 1), 2)
    return prev, nxt


def _bwd_states(rest, gates, cw, gb, lgb, tm):
    B, Lp, _ = rest.shape
    nt = Lp // tm
    nchunk = tm // BLOCK
    rev = lambda f: (lambda b, i: f(b, nt - 1 - i))
    prev, nxt = _halo_specs(tm, nt)
    const2 = lambda b, i: (0, 0)
    state_spec = lambda r, c: pl.BlockSpec((1, nchunk, r, c), lambda b, i: (b, nt - 1 - i, 0, 0))
    nc = Lp // BLOCK
    return pl.pallas_call(
        functools.partial(_bwd_state_kernel, nchunk=nchunk),
        grid=(B, nt),
        in_specs=[
            pl.BlockSpec((1, tm, REST_W), rev(lambda b, t: (b, t, 0))),
            pl.BlockSpec((1, 8, 2 * HW), rev(prev)),
            pl.BlockSpec((1, 8, 2 * HW), rev(nxt)),
            pl.BlockSpec((1, tm, BLOCK), rev(lambda b, t: (b, t, 0))),
            pl.BlockSpec((3, 2 * HW), const2),
            pl.BlockSpec((1, BLOCK), const2),
            pl.BlockSpec((1, HW), const2),
        ],
        out_specs=[state_spec(HW, HW), state_spec(HW, HW), state_spec(HW, BLOCK), state_spec(8, BLOCK)],
        out_shape=[
            jax.ShapeDtypeStruct((B, nc, HW, HW), bf16),
            jax.ShapeDtypeStruct((B, nc, HW, HW), bf16),
            jax.ShapeDtypeStruct((B, nc, HW, BLOCK), bf16),
            jax.ShapeDtypeStruct((B, nc, 8, BLOCK), f32),
        ],
        scratch_shapes=[
            pltpu.VMEM((HW, HW), f32), pltpu.VMEM((HW, HW), f32), pltpu.VMEM((HW, BLOCK), f32),
            pltpu.VMEM((8, BLOCK), f32), pltpu.VMEM((BLOCK, HW), f32), pltpu.VMEM((BLOCK, HW), f32),
        ],
        compiler_params=pltpu.CompilerParams(
            dimension_semantics=("parallel", "arbitrary"), vmem_limit_bytes=VMEM_LIMIT),
        name="bwd_states",
    )(rest, rest, rest, gates, cw, gb, lgb)


def _fwd_sweep(rest, gates, rb, sb, nbw, mb, cw, gb, lgf, lgb, lgf4, lgb4, rnw, mnw, tm):
    B, Lp, _ = rest.shape
    nt = Lp // tm
    nchunk = tm // BLOCK
    prev, nxt = _halo_specs(tm, nt)
    const2 = lambda b, t: (0, 0)
    state_spec = lambda r, c: pl.BlockSpec((1, nchunk, r, c), lambda b, t: (b, t, 0, 0))
    return pl.pallas_call(
        functools.partial(_fwd_kernel, nchunk=nchunk),
        grid=(B, nt),
        in_specs=[
            pl.BlockSpec((1, tm, REST_W), lambda b, t: (b, t, 0)),
            pl.BlockSpec((1, 8, 2 * HW), prev),
            pl.BlockSpec((1, 8, 2 * HW), nxt),
            pl.BlockSpec((1, tm, BLOCK), lambda b, t: (b, t, 0)),
            state_spec(HW, HW), state_spec(HW, HW), state_spec(HW, BLOCK), state_spec(8, BLOCK),
            pl.BlockSpec((3, 2 * HW), const2),
            pl.BlockSpec((1, BLOCK), const2),
            pl.BlockSpec((1, HW), const2),
            pl.BlockSpec((1, HW), const2),
            pl.BlockSpec((1, 4 * BLOCK), const2),
            pl.BlockSpec((1, 4 * BLOCK), const2),
            pl.BlockSpec((1, HW), const2),
            pl.BlockSpec((1, HW), const2),
        ],
        out_specs=pl.BlockSpec((1, tm, 2 * HW), lambda b, t: (b, t, 0)),
        out_shape=jax.ShapeDtypeStruct((B, Lp, 2 * HW), bf16),
        scratch_shapes=[
            pltpu.VMEM((HW, HW), f32), pltpu.VMEM((HW, HW), f32), pltpu.VMEM((HW, BLOCK), f32),
            pltpu.VMEM((8, BLOCK), f32),
            pltpu.VMEM((BLOCK, HW), f32), pltpu.VMEM((BLOCK, HW), f32),
            pltpu.VMEM((BLOCK, HW), f32), pltpu.VMEM((BLOCK, HW), f32),
            pltpu.VMEM((BLOCK, 4 * BLOCK), f32),
        ],
        compiler_params=pltpu.CompilerParams(
            dimension_semantics=("parallel", "arbitrary"), vmem_limit_bytes=VMEM_LIMIT),
        name="fwd_sweep",
    )(rest, rest, rest, gates, rb, sb, nbw, mb, cw, gb, lgf, lgb, lgf4, lgb4, rnw, mnw)


def _out_kernel(h_ref, a_ref, rm_ref, w_ref, o_ref):
    y = h_ref[0] + (_dot(a_ref[0], w_ref[0:ATT_W, :]) + _dot(rm_ref[0], w_ref[ATT_W:, :]))
    rows = lax.broadcasted_iota(jnp.int32, (y.shape[0], 1), 0)
    keep = jnp.logical_or(pl.program_id(1) > 0, rows >= PAD)
    o_ref[0] = jnp.where(keep, y, 0.0)


def _out_proj(h, att_o, rm, w_out, tm):
    B, Lp, _ = h.shape
    return pl.pallas_call(
        _out_kernel,
        grid=(B, Lp // tm),
        in_specs=[
            pl.BlockSpec((1, tm, D_MODEL), lambda b, i: (b, i, 0)),
            pl.BlockSpec((1, tm, ATT_W), lambda b, i: (b, i, 0)),
            pl.BlockSpec((1, tm, 2 * HW), lambda b, i: (b, i, 0)),
            pl.BlockSpec((D_MODEL, D_MODEL), lambda b, i: (0, 0)),
        ],
        out_specs=pl.BlockSpec((1, tm, D_MODEL), lambda b, i: (b, i, 0)),
        out_shape=jax.ShapeDtypeStruct(h.shape, f32),
        compiler_params=pltpu.CompilerParams(
            dimension_semantics=("parallel", "parallel"), vmem_limit_bytes=VMEM_LIMIT),
        name="out_proj",
    )(h, att_o, rm, w_out)


def _ffn_kernel(h_ref, hp_ref, hn_ref, n2_ref, wup_ref, cw_ref, wdn_ref, o_ref, act_ref):
    i = pl.program_id(1)
    last = i == pl.num_programs(1) - 1
    nw = n2_ref[...]

    def norm(x):
        ms = jnp.mean(x * x, axis=-1, keepdims=True)
        return ((x * lax.rsqrt(ms + EPS)) * nw).astype(bf16)

    x = h_ref[0]
    tm = x.shape[0]
    xe = norm(jnp.concatenate([x, hp_ref[0], jnp.where(last, 0.0, hn_ref[0])], axis=0))
    rows = lax.broadcasted_iota(jnp.int32, (tm, 1), 0)
    first_row = rows == 0
    last_row = rows == tm - 1

    def conv(ue, w3):
        u = ue[0:tm]
        up = jnp.where(first_row, ue[tm + 7:tm + 8, :], pltpu.roll(u, 1, 0))
        dn = jnp.where(last_row, ue[tm + 8:tm + 9, :], pltpu.roll(u, tm - 1, 0))
        return (up * w3[0:1, :] + u * w3[1:2, :]) + dn * w3[2:3, :]

    nck = D_FF // FF_CHUNK

    def up(c):
        gs = slice(c * FF_CHUNK, (c + 1) * FF_CHUNK)
        vs = slice(D_FF + c * FF_CHUNK, D_FF + (c + 1) * FF_CHUNK)
        return _dot(xe, wup_ref[:, gs]), _dot(xe, wup_ref[:, vs])

    u_next = up(0)
    acc = None
    done = 0
    for c in range(nck):
        ug, uv = u_next
        if c + 1 < nck:
            u_next = up(c + 1)
        gs = slice(c * FF_CHUNK, (c + 1) * FF_CHUNK)
        vs = slice(D_FF + c * FF_CHUNK, D_FF + (c + 1) * FF_CHUNK)
        g = conv(ug, cw_ref[:, gs])
        v = conv(uv, cw_ref[:, vs])
        act_ref[:, gs] = ((g * _sigmoid(g)) * v).astype(bf16)
        if (c + 1) % DOWN_GROUP == 0 or c + 1 == nck:
            ks = slice(done * FF_CHUNK, (c + 1) * FF_CHUNK)
            part = _dot(act_ref[:, ks], wdn_ref[ks, :])
            acc = part if acc is None else acc + part
            done = c + 1
    y = x + acc
    keep = jnp.logical_or(i > 0, rows >= PAD)
    o_ref[0] = jnp.where(keep, y, 0.0)


def _ffn(h, n2, w_up, cw, w_dn, tm):
    B, Lp, _ = h.shape
    nt = Lp // tm
    r8 = tm // 8
    return pl.pallas_call(
        _ffn_kernel,
        grid=(B, nt),
        in_specs=[
            pl.BlockSpec((1, tm, D_MODEL), lambda b, i: (b, i, 0)),
            pl.BlockSpec((1, 8, D_MODEL), lambda b, i: (b, jnp.maximum(i * r8 - 1, 0), 0)),
            pl.BlockSpec((1, 8, D_MODEL), lambda b, i: (b, jnp.minimum((i + 1) * r8, nt * r8 - 1), 0)),
            pl.BlockSpec((1, D_MODEL), lambda b, i: (0, 0)),
            pl.BlockSpec((D_MODEL, 2 * D_FF), lambda b, i: (0, 0), pipeline_mode=pl.Buffered(1)),
            pl.BlockSpec((3, 2 * D_FF), lambda b, i: (0, 0)),
            pl.BlockSpec((D_FF, D_MODEL), lambda b, i: (0, 0), pipeline_mode=pl.Buffered(1)),
        ],
        out_specs=pl.BlockSpec((1, tm, D_MODEL), lambda b, i: (b, i, 0)),
        out_shape=jax.ShapeDtypeStruct(h.shape, f32),
        scratch_shapes=[pltpu.VMEM((tm, D_FF), bf16)],
        compiler_params=pltpu.CompilerParams(
            dimension_semantics=("parallel", "parallel"), vmem_limit_bytes=VMEM_LIMIT),
        name="ffn",
    )(h, h, h, n2, w_up, cw, w_dn)


def _lane_groups(x, width):
    return jnp.repeat(x, width, axis=-1)[:, None, :]


def _prepare_params(norm1_w, w_in, attn_q_norm_w, attn_k_norm_w, attn_sink, ret_decay_logit,
                    ret_norm_w, mlstm_conv_w, mlstm_gate_b, mlstm_norm_w, w_out, norm2_w,
                    ffn_up, ffn_conv_w, ffn_down):
    scale = HEAD_DIM ** -0.5
    qrow = jnp.tile(attn_q_norm_w.astype(f32) * scale, (1, 2))
    krow = jnp.tile(attn_k_norm_w.astype(f32), (1, 2))
    qkw = jnp.concatenate([jnp.stack([qrow] * 4 + [krow], axis=1),
                           jnp.zeros((DEPTH, 3, BLOCK), f32)], axis=1)
    sink = jnp.repeat(attn_sink.astype(f32).reshape(DEPTH, ATT_KV_HEADS, 4), BLOCK, axis=-1)
    log_gamma = jax.nn.log_sigmoid(ret_decay_logit.astype(f32))
    gb = jnp.concatenate([mlstm_gate_b.astype(f32), jnp.zeros((DEPTH, BLOCK - N_GATES), f32)], axis=-1)
    return dict(
        n1=norm1_w.astype(f32)[:, None, :],
        w_main=w_in[:, :, :MAIN_W].astype(bf16),
        w_gate=jnp.concatenate([w_in[:, :, MAIN_W:], jnp.zeros((DEPTH, D_MODEL, BLOCK - N_GATES), w_in.dtype)],
                               axis=-1).astype(bf16),
        qkw=qkw,
        sink=sink[..., None],
        lgf=_lane_groups(log_gamma[:, 0], HEAD_DIM),
        lgb=_lane_groups(log_gamma[:, 1], HEAD_DIM),
        lgf4=_lane_groups(log_gamma[:, 0], BLOCK),
        lgb4=_lane_groups(log_gamma[:, 1], BLOCK),
        rnw=ret_norm_w.astype(f32)[:, None, :],
        cw=mlstm_conv_w.astype(f32),
        gb=gb[:, None, :],
        mnw=mlstm_norm_w.astype(f32)[:, None, :],
        w_out=w_out.astype(bf16),
        n2=norm2_w.astype(f32)[:, None, :],
        w_up=ffn_up.astype(bf16),
        fcw=ffn_conv_w.astype(f32),
        w_dn=ffn_down.astype(bf16),
    )


def _trunk(x, meta_tokens, params, bias):
    B, seq, _ = x.shape
    lp = seq + BLOCK
    tm = _row_tile(lp)
    meta = jnp.broadcast_to(meta_tokens[None].astype(x.dtype), (B, N_META, D_MODEL))
    h = jnp.concatenate([jnp.zeros((B, PAD, D_MODEL), x.dtype), meta, x], axis=1)

    def layer(h, p):
        att, rest, gates = _project(h, p["n1"], p["w_main"], p["w_gate"], p["qkw"], tm)
        att_o = _attention(att, bias, p["sink"], tm)
        rb, sb, nbw, mb = _bwd_states(rest, gates, p["cw"], p["gb"], p["lgb"], tm)
        rm = _fwd_sweep(rest, gates, rb, sb, nbw, mb, p["cw"], p["gb"], p["lgf"], p["lgb"],
                        p["lgf4"], p["lgb4"], p["rnw"], p["mnw"], tm)
        h = _out_proj(h, att_o, rm, p["w_out"], tm)
        h = _ffn(h, p["n2"], p["w_up"], p["fcw"], p["w_dn"], tm)
        return h, None

    h, _ = lax.scan(layer, h, params)
    return h[:, BLOCK:]


def kernel(x_prompt, x_sample, meta_tokens, norm1_w, w_in, attn_q_norm_w, attn_k_norm_w, attn_sink,
           ret_decay_logit, ret_norm_w, mlstm_conv_w, mlstm_gate_b, mlstm_norm_w, w_out, norm2_w,
           ffn_up, ffn_conv_w, ffn_down):
    params = _prepare_params(norm1_w, w_in, attn_q_norm_w, attn_k_norm_w, attn_sink, ret_decay_logit,
                             ret_norm_w, mlstm_conv_w, mlstm_gate_b, mlstm_norm_w, w_out, norm2_w,
                             ffn_up, ffn_conv_w, ffn_down)
    bias = _attention_bias()
    y_prompt = _trunk(x_prompt, meta_tokens, params, bias)
    y_sample = _trunk(x_sample, meta_tokens, params, bias)
    return (y_prompt, y_sample)
```

```python
import functools
import math

import jax
import jax.numpy as jnp
from jax import lax
from jax.experimental import pallas as pl
from jax.experimental.pallas import tpu as pltpu

f32 = jnp.float32
bf16 = jnp.bfloat16

D_MODEL = 1024
DEPTH = 4
N_META = 16
BLOCK = 128
PAD = BLOCK - N_META
HEAD_DIM = 64
ATT_HEADS = 8
ATT_KV_HEADS = 2
ATT_W = 512
KV_W = 128
HW = 256
N_GATES = 16
MAIN_W = 2816
REST_W = 2048
ATT_OUT_W = 1024
D_FF = 2816
FF_CHUNK = 256
DOWN_GROUP = 4
EPS = 1e-6
NEG = -1e30
VMEM_LIMIT = 56 * 1024 * 1024


def _row_tile(lp):
    nb = lp // BLOCK
    best = 1
    for t in range(1, 7):
        if nb % t == 0:
            best = t
    return best * BLOCK


def _sigmoid(x):
    return 1.0 / (1.0 + jnp.exp(-x))


def _log_sigmoid(x):
    return jnp.minimum(x, 0.0) - jnp.log(1.0 + jnp.exp(-jnp.abs(x)))


def _seg64_mean(x, lo):
    a = jnp.sum(jnp.where(lo, x, 0.0), axis=-1, keepdims=True)
    b = jnp.sum(jnp.where(lo, 0.0, x), axis=-1, keepdims=True)
    return jnp.where(lo, a, b) * (1.0 / HEAD_DIM)


def _block_stack(x_b):
    lane = lax.broadcasted_iota(jnp.int32, (1, HW), 1)
    parts = []
    for h in range(4):
        keep = (lane >= h * HEAD_DIM) & (lane < (h + 1) * HEAD_DIM)
        parts.append(jnp.where(keep, x_b, jnp.zeros_like(x_b)))
    return jnp.concatenate(parts, axis=0)


def _bd_mask():
    r = lax.broadcasted_iota(jnp.int32, (HW, HW), 0) >> 6
    c = lax.broadcasted_iota(jnp.int32, (HW, HW), 1) >> 6
    return r == c


def _dot(a, b):
    return jnp.dot(a, b, preferred_element_type=f32)


def _dot_nt(a, b):
    return lax.dot_general(a, b, (((1,), (1,)), ((), ())), preferred_element_type=f32)


def _split3(x):
    p1 = x.astype(bf16)
    r1 = x - p1.astype(f32)
    p2 = r1.astype(bf16)
    p3 = (r1 - p2.astype(f32)).astype(bf16)
    return p1, p2, p3


def _dot01_right(x, m01):
    p1, p2, p3 = _split3(x)
    return (_dot(p3, m01) + _dot(p2, m01)) + _dot(p1, m01)


def _proj_kernel(h_ref, n1_ref, w_ref, wg_ref, qkw_ref, att_ref, rest_ref, gate_ref):
    x = h_ref[0]
    ms = jnp.mean(x * x, axis=-1, keepdims=True)
    hn = ((x * lax.rsqrt(ms + EPS)) * n1_ref[...]).astype(bf16)
    pa = _dot(hn, w_ref[:, 0:ATT_W + 2 * KV_W])
    lo = lax.broadcasted_iota(jnp.int32, (1, BLOCK), 1) < HEAD_DIM
    pieces = []
    for c in range(5):
        xc = pa[:, c * BLOCK:(c + 1) * BLOCK]
        msq = _seg64_mean(xc * xc, lo)
        pieces.append((xc * lax.rsqrt(msq + EPS)) * qkw_ref[c:c + 1, :])
    k = pieces[4]
    v = pa[:, ATT_W + KV_W:ATT_W + 2 * KV_W]
    k_sw = pltpu.roll(k, HEAD_DIM, 1)
    v_sw = pltpu.roll(v, HEAD_DIM, 1)
    out = pieces[:4] + [jnp.where(lo, k, k_sw), jnp.where(lo, k_sw, k),
                        jnp.where(lo, v, v_sw), jnp.where(lo, v_sw, v)]
    att_ref[0] = jnp.concatenate(out, axis=-1).astype(bf16)
    rest_ref[0] = _dot(hn, w_ref[:, ATT_W + 2 * KV_W:MAIN_W])
    gate_ref[0] = _dot(hn, wg_ref[...])


def _project(h, n1, w_main, w_gate, qkw, tm):
    B, Lp, _ = h.shape
    grid = (B, Lp // tm)
    return pl.pallas_call(
        _proj_kernel,
        grid=grid,
        in_specs=[
            pl.BlockSpec((1, tm, D_MODEL), lambda b, i: (b, i, 0)),
            pl.BlockSpec((1, D_MODEL), lambda b, i: (0, 0)),
            pl.BlockSpec((D_MODEL, MAIN_W), lambda b, i: (0, 0)),
            pl.BlockSpec((D_MODEL, BLOCK), lambda b, i: (0, 0)),
            pl.BlockSpec((8, BLOCK), lambda b, i: (0, 0)),
        ],
        out_specs=[
            pl.BlockSpec((1, tm, ATT_OUT_W), lambda b, i: (b, i, 0)),
            pl.BlockSpec((1, tm, REST_W), lambda b, i: (b, i, 0)),
            pl.BlockSpec((1, tm, BLOCK), lambda b, i: (b, i, 0)),
        ],
        out_shape=[
            jax.ShapeDtypeStruct((B, Lp, ATT_OUT_W), bf16),
            jax.ShapeDtypeStruct((B, Lp, REST_W), f32),
            jax.ShapeDtypeStruct((B, Lp, BLOCK), f32),
        ],
        compiler_params=pltpu.CompilerParams(
            dimension_semantics=("parallel", "parallel"), vmem_limit_bytes=VMEM_LIMIT),
        name="proj",
    )(h, n1, w_main, w_gate, qkw)


def _attn_kernel(q_ref, kvp_ref, kvc_ref, kvn_ref, kvm_ref, bias_ref, sink_ref, o_ref, *, nblk):
    i = pl.program_id(1)
    nb = pl.num_programs(1) * nblk
    lo = lax.broadcasted_iota(jnp.int32, (1, BLOCK), 1) < HEAD_DIM
    top = lax.broadcasted_iota(jnp.int32, (BLOCK, 1), 0) < HEAD_DIM
    zpad = jnp.zeros((BLOCK - N_META, BLOCK), bf16)
    kvs = ([kvp_ref[0]] + [kvc_ref[0, u * BLOCK:(u + 1) * BLOCK, :] for u in range(nblk)] + [kvn_ref[0]])
    meta = kvm_ref[0]

    def v_t(blk, kh):
        return blk[:, (2 + kh) * BLOCK:(3 + kh) * BLOCK].astype(f32).T.astype(bf16)

    vts = [[v_t(blk, kh) for blk in kvs] for kh in range(ATT_KV_HEADS)]
    vt_meta = [v_t(jnp.concatenate([meta, jnp.zeros((BLOCK - N_META, 4 * BLOCK), bf16)], axis=0), kh)
               for kh in range(ATT_KV_HEADS)]

    def scores(t, kh):
        j = i * nblk + t
        var = jnp.where(j == 0, 1, jnp.where(j == 1, 2, jnp.where(j == nb - 1, 3, 0)))
        q = q_ref[0, t * BLOCK:(t + 1) * BLOCK, :]
        ksl = slice(kh * BLOCK, (kh + 1) * BLOCK)
        kcat = jnp.concatenate([kvs[t][:, ksl], kvs[t + 1][:, ksl], kvs[t + 2][:, ksl],
                                meta[:, ksl], zpad], axis=0)
        qparts = []
        for g in range(4):
            grp = q[:, (kh * 2 + g // 2) * BLOCK:(kh * 2 + g // 2 + 1) * BLOCK]
            keep = lo if g % 2 == 0 else jnp.logical_not(lo)
            qparts.append(jnp.where(keep, grp, jnp.zeros_like(grp)))
        qg = jnp.concatenate(qparts, axis=0)
        return _dot_nt(kcat, qg) + bias_ref[var * ATT_KV_HEADS + kh]

    units = [(t, kh) for t in range(nblk) for kh in range(ATT_KV_HEADS)]
    s_next = scores(*units[0])
    outs = []
    for n, (t, kh) in enumerate(units):
        s = s_next
        if n + 1 < len(units):
            s_next = scores(*units[n + 1])
        vt_cat = jnp.concatenate([vts[kh][t], vts[kh][t + 1], vts[kh][t + 2], vt_meta[kh]], axis=1)
        sk = sink_ref[kh]
        m = jnp.maximum(jnp.max(s, axis=0, keepdims=True), sk)
        p = jnp.exp(s - m)
        den = jnp.sum(p, axis=0, keepdims=True) + jnp.exp(sk - m)
        o_t = _dot(vt_cat, p.astype(bf16)) * (1.0 / den)
        for c in range(2):
            pair = jnp.where(top, o_t[:, (2 * c) * BLOCK:(2 * c + 1) * BLOCK],
                             o_t[:, (2 * c + 1) * BLOCK:(2 * c + 2) * BLOCK])
            outs.append(pair.T)
        if kh == ATT_KV_HEADS - 1:
            o_ref[0, t * BLOCK:(t + 1) * BLOCK, :] = jnp.concatenate(outs, axis=-1).astype(bf16)
            outs = []


def _attention(att, bias, sink, tm):
    B, Lp, _ = att.shape
    nblk = tm // BLOCK
    nt = Lp // tm
    assert Lp // BLOCK >= 4
    meta_row = PAD // N_META
    return pl.pallas_call(
        functools.partial(_attn_kernel, nblk=nblk),
        grid=(B, nt),
        in_specs=[
            pl.BlockSpec((1, tm, ATT_W), lambda b, i: (b, i, 0)),
            pl.BlockSpec((1, BLOCK, ATT_W), lambda b, i: (b, jnp.maximum(i * nblk - 1, 0), 1)),
            pl.BlockSpec((1, tm, ATT_W), lambda b, i: (b, i, 1)),
            pl.BlockSpec((1, BLOCK, ATT_W), lambda b, i: (b, jnp.minimum((i + 1) * nblk, nt * nblk - 1), 1)),
            pl.BlockSpec((1, N_META, ATT_W), lambda b, i: (b, meta_row, 1)),
            pl.BlockSpec((4 * ATT_KV_HEADS, 4 * BLOCK, 4 * BLOCK), lambda b, i: (0, 0, 0),
                         pipeline_mode=pl.Buffered(1)),
            pl.BlockSpec((ATT_KV_HEADS, 1, 4 * BLOCK), lambda b, i: (0, 0, 0)),
        ],
        out_specs=pl.BlockSpec((1, tm, ATT_W), lambda b, i: (b, i, 0)),
        out_shape=jax.ShapeDtypeStruct((B, Lp, ATT_W), bf16),
        compiler_params=pltpu.CompilerParams(
            dimension_semantics=("parallel", "parallel"), vmem_limit_bytes=VMEM_LIMIT),
        name="attn",
    )(att, att, att, att, att, bias, sink)


def _attention_bias():
    i = jnp.arange(BLOCK)[:, None]
    c = jnp.arange(4 * BLOCK)[None, :]
    dist = jnp.abs(i + BLOCK - c)
    band = (c < 3 * BLOCK) & (dist <= BLOCK)
    meta = (c >= 3 * BLOCK) & (c < 3 * BLOCK + N_META)
    slopes = jnp.exp2(-8.0 * jnp.arange(1, ATT_HEADS + 1, dtype=f32) / ATT_HEADS)
    blk = c // BLOCK
    variants = []
    for dead in ((), (0, 1), (0,), (2,)):
        ok = band
        for d in dead:
            ok = ok & (blk != d)
        b = jnp.where(ok[None], -(slopes[:, None, None] * dist[None].astype(f32)),
                      jnp.where(meta[None], 0.0, NEG))
        variants.append(b.reshape(ATT_KV_HEADS, 4 * BLOCK, 4 * BLOCK).transpose(0, 2, 1))
    return jnp.concatenate(variants, axis=0).astype(f32)


def _conv3_silu(x, prev8, nxt8, w3):
    tm = x.shape[0]
    rows = lax.broadcasted_iota(jnp.int32, (tm, 1), 0)
    up = jnp.where(rows == 0, prev8[7:8, :], pltpu.roll(x, 1, 0))
    dn = jnp.where(rows == tm - 1, nxt8[0:1, :], pltpu.roll(x, tm - 1, 0))
    y = (up * w3[0:1, :] + x * w3[1:2, :]) + dn * w3[2:3, :]
    return y * _sigmoid(y)


def _rows_to_heads(rows):
    return jnp.concatenate([jnp.broadcast_to(r, (HEAD_DIM, r.shape[1])) for r in rows], axis=0)


def _bd_lanes(xT_b):
    row_head = lax.broadcasted_iota(jnp.int32, (HW, 1), 0) >> 6
    return jnp.concatenate([jnp.where(row_head == h, xT_b, jnp.zeros_like(xT_b)) for h in range(4)], axis=1)


def _gate_rows(gates, gbias):
    gt = (gates + gbias).T[0:16, :]
    grow = lax.broadcasted_iota(jnp.int32, (16, 1), 0)
    PT = jnp.where((grow & 4) != 0, _log_sigmoid(gt), gt)
    r = lax.broadcasted_iota(jnp.int32, (BLOCK, 2 * BLOCK), 0)
    c = lax.broadcasted_iota(jnp.int32, (BLOCK, 2 * BLOCK), 1)
    tri = ((c < BLOCK) & (r <= c)) | ((c >= BLOCK) & (r >= c - BLOCK))
    cum2 = _dot01_right(PT, jnp.where(tri, 1.0, 0.0).astype(bf16))
    row = lax.broadcasted_iota(jnp.int32, (16, 1), 0)
    cum = jnp.where(row < 8, cum2[:, 0:BLOCK], cum2[:, BLOCK:2 * BLOCK])
    cum_al = pltpu.roll(cum, 12, 0)
    total = jnp.where(row < 8, cum_al[:, BLOCK - 1:BLOCK], cum_al[:, 0:1])
    return PT, cum_al, total


def _mlstm_state_step(r0, PT, cum_al, total, kT_b, vT, S_ref, N_ref, M_ref, bd):
    wend = (total - cum_al) + PT
    m_loc = jnp.max(wend, axis=-1, keepdims=True)
    e_end = jnp.exp(wend - m_loc)
    m_old = M_ref[:, 0:1]
    m_new = jnp.maximum(total + m_old, m_loc)
    a_rep = jnp.broadcast_to(jnp.exp((total + m_old) - m_new), (16, BLOCK))
    c_rep = jnp.broadcast_to(jnp.exp(m_loc - m_new), (16, BLOCK))
    M_ref[...] = jnp.broadcast_to(m_new, (16, BLOCK))
    e_heads = _rows_to_heads([e_end[r0 + h:r0 + h + 1, :] for h in range(4)])
    upd = _dot_nt((vT * e_heads).astype(bf16), kT_b)
    a_m = _rows_to_heads([a_rep[r0 + h:r0 + h + 1, :] for h in range(4)])
    c_m = _rows_to_heads([c_rep[r0 + h:r0 + h + 1, :] for h in range(4)])
    a_m2 = jnp.concatenate([a_m, a_m], axis=1)
    c_m2 = jnp.concatenate([c_m, c_m], axis=1)
    S_ref[...] = a_m2 * S_ref[...] + c_m2 * jnp.where(bd, upd, 0.0)
    n_loc = _dot_nt(e_end.astype(bf16), kT_b)
    lane_head = lax.broadcasted_iota(jnp.int32, (8, HW), 1) >> 6
    row8 = lax.broadcasted_iota(jnp.int32, (8, HW), 0)
    a8 = jnp.concatenate([a_rep[r0:r0 + 8, :], a_rep[r0:r0 + 8, :]], axis=1)
    c8 = jnp.concatenate([c_rep[r0:r0 + 8, :], c_rep[r0:r0 + 8, :]], axis=1)
    N_ref[0:8, :] = a8 * N_ref[0:8, :] + c8 * jnp.where(lane_head == row8, n_loc[r0:r0 + 8, :], 0.0)


def _mlstm_inputs(rest_ref, prev_ref, next_ref, cw_ref, first, last):
    x = rest_ref[0, :, 4 * HW:6 * HW]
    tm = x.shape[0]
    prev8 = jnp.where(first, 0.0, prev_ref[0])
    nxt8 = jnp.where(last, 0.0, next_ref[0])
    y = _conv3_silu(x, prev8, nxt8, cw_ref[...])
    rows = lax.broadcasted_iota(jnp.int32, (tm, 1), 0)
    keep = jnp.logical_or(jnp.logical_not(first), rows >= PAD)
    mq = y[:, 0:HW]
    mk = jnp.where(keep, y[:, HW:2 * HW], 0.0) * (HEAD_DIM ** -0.5)
    return mq, mk


def _bwd_state_kernel(rk_ref, rv_ref, mkx_ref, mv_ref, prev_ref, next_ref, gate_ref, cw_ref, gb_ref, lgb_ref,
                      r_out, s_out, n_out, m_out,
                      R_ref, S_ref, N_ref, M_ref, dk_ref, *, nchunk):
    i = pl.program_id(1)
    nt = pl.num_programs(1)
    ti = nt - 1 - i

    @pl.when(i == 0)
    def _():
        R_ref[...] = jnp.zeros_like(R_ref)
        S_ref[...] = jnp.zeros_like(S_ref)
        N_ref[...] = jnp.zeros_like(N_ref)
        M_ref[...] = jnp.zeros_like(M_ref)
        pos = lax.broadcasted_iota(jnp.int32, (1, BLOCK), 1).astype(f32)
        dk_ref[...] = jnp.exp(lgb_ref[...] * pos)

    bd = _bd_mask()
    g1 = jnp.exp(lgb_ref[...] * float(BLOCK))
    g_blk = jnp.concatenate([g1, g1], axis=1)
    first = ti == 0
    y = _conv3_silu(mkx_ref[0], jnp.where(first, 0.0, prev_ref[0]), jnp.where(ti == nt - 1, 0.0, next_ref[0]),
                    cw_ref[:, HW:2 * HW])
    trow = lax.broadcasted_iota(jnp.int32, (y.shape[0], 1), 0)
    mk_all = jnp.where(jnp.logical_or(jnp.logical_not(first), trow >= PAD), y, 0.0) * (HEAD_DIM ** -0.5)

    def prep(c):
        rows = slice(c * BLOCK, (c + 1) * BLOCK)
        kT_b = (rk_ref[0, rows, :] * (HEAD_DIM ** -0.5)).T.astype(bf16)
        vT = rv_ref[0, rows, :].T
        mkT_b = mk_all[rows].T.astype(bf16)
        mvT = mv_ref[0, rows, :].T
        return kT_b, vT, mkT_b, mvT, _gate_rows(gate_ref[0, rows, :], gb_ref[...])

    order = list(reversed(range(nchunk)))
    nxt = prep(order[0])
    for n, c in enumerate(order):
        kT_b, vT, mkT_b, mvT, (PT, cum_al, total) = nxt
        if n + 1 < nchunk:
            nxt = prep(order[n + 1])
        r_out[0, c] = R_ref[...].astype(bf16)
        s_out[0, c] = S_ref[...].astype(bf16)
        n_out[0, c] = N_ref[...].astype(bf16)
        m_out[0, c] = M_ref[...]
        R_ref[...] = R_ref[...] * g_blk + jnp.where(bd, _dot_nt((vT * dk_ref[...]).astype(bf16), kT_b), 0.0)
        _mlstm_state_step(8, PT, cum_al, total, mkT_b, mvT, S_ref, N_ref, M_ref, bd)


def _head_layernorm_t(x, w_t):
    outs = []
    for h in range(4):
        xh = x[h * HEAD_DIM:(h + 1) * HEAD_DIM, :]
        mu = jnp.mean(xh, axis=0, keepdims=True)
        d = xh - mu
        var = jnp.mean(d * d, axis=0, keepdims=True)
        outs.append(d * lax.rsqrt(var + EPS))
    return jnp.concatenate(outs, axis=0) * w_t


def _mlstm_direction(r0, mask, s_t, PT, cum_al, m_prev, vT_bd, S_b, N_b, q_b):
    qn = _dot_nt(N_b, q_b)
    ws, a_rows, r_rows = [], [], []
    for h in range(4):
        rr = r0 + h
        b_row = cum_al[rr:rr + 1, :]
        cb = jnp.broadcast_to(PT[rr:rr + 1, :] - b_row, (BLOCK, BLOCK)).T
        dl = jnp.where(mask, b_row + cb, NEG)
        m_intra = jnp.max(dl, axis=0, keepdims=True)
        inter = b_row + m_prev[rr:rr + 1, :]
        m_t = jnp.maximum(m_intra, inter)
        a_i = jnp.exp(inter - m_t)
        w = jnp.exp(dl - m_t) * s_t[h * BLOCK:(h + 1) * BLOCK, :]
        den = jnp.sum(w, axis=0, keepdims=True) + a_i * qn[h:h + 1, :]
        r_rows.append(1.0 / jnp.maximum(jnp.abs(den), jnp.exp(-m_t)))
        a_rows.append(a_i)
        ws.append(w.astype(bf16))
    num = _dot(vT_bd, jnp.concatenate(ws, axis=0)) + _rows_to_heads(a_rows) * _dot_nt(S_b, q_b)
    return num * _rows_to_heads(r_rows)


def _fwd_kernel(rest_ref, prev_ref, next_ref, gate_ref, rb_ref, sb_ref, nb_ref, mb_ref,
                cw_ref, gb_ref, lgf_ref, lgb_ref, lgf4_ref, lgb4_ref, rnw_ref, mnw_ref,
                o_ref,
                R_ref, S_ref, N_ref, M_ref, dk_ref, dqf_ref, dqb_ref, dcomb_ref,
                *, nchunk):
    i = pl.program_id(1)
    nt = pl.num_programs(1)

    @pl.when(i == 0)
    def _():
        R_ref[...] = jnp.zeros_like(R_ref)
        S_ref[...] = jnp.zeros_like(S_ref)
        N_ref[...] = jnp.zeros_like(N_ref)
        M_ref[...] = jnp.zeros_like(M_ref)
        pos = lax.broadcasted_iota(jnp.int32, (1, BLOCK), 1).astype(f32)
        dk_ref[...] = jnp.exp(lgf_ref[...] * ((BLOCK - 1.0) - pos))
        dqf_ref[...] = jnp.exp(lgf_ref[...] * (pos + 1.0))
        dqb_ref[...] = jnp.exp(lgb_ref[...] * (float(BLOCK) - pos))
        key = lax.broadcasted_iota(jnp.int32, (4 * BLOCK, BLOCK), 0) & (BLOCK - 1)
        qry = lax.broadcasted_iota(jnp.int32, (4 * BLOCK, BLOCK), 1)
        d = (qry - key).astype(f32)
        dcomb_ref[...] = (jnp.where(d >= 0, jnp.exp(lgf4_ref[...] * jnp.maximum(d, 0.0)), 0.0)
                          + jnp.where(d <= 0, jnp.exp(lgb4_ref[...] * jnp.maximum(-d, 0.0)), 0.0))

    bd = _bd_mask()
    g1 = jnp.exp(lgf_ref[...] * float(BLOCK))
    g_blk = jnp.concatenate([g1, g1], axis=1)
    key_i = lax.broadcasted_iota(jnp.int32, (BLOCK, BLOCK), 0)
    qry_i = lax.broadcasted_iota(jnp.int32, (BLOCK, BLOCK), 1)
    mq_all, mk_all = _mlstm_inputs(rest_ref, prev_ref, next_ref, cw_ref, i == 0, i == nt - 1)

    def prep(c):
        rows = slice(c * BLOCK, (c + 1) * BLOCK)
        q_b = rest_ref[0, rows, 0:HW].astype(bf16)
        k = rest_ref[0, rows, HW:2 * HW] * (HEAD_DIM ** -0.5)
        k_b = k.astype(bf16)
        kT_b = k.T.astype(bf16)
        vT = rest_ref[0, rows, 2 * HW:3 * HW].T
        a_t = (_dot_nt(_block_stack(k_b), q_b) * dcomb_ref[...]).astype(bf16)
        ro_t = _dot(_bd_lanes(vT.astype(bf16)), a_t)
        mq_b = mq_all[rows].astype(bf16)
        mk = mk_all[rows]
        mkT_b = mk.T.astype(bf16)
        mvT = rest_ref[0, rows, 6 * HW:7 * HW].T
        ms_t = _dot_nt(_block_stack(mk.astype(bf16)), mq_b)
        gr = _gate_rows(gate_ref[0, rows, :], gb_ref[...])
        return q_b, kT_b, vT, ro_t, mq_b, mkT_b, mvT, _bd_lanes(mvT.astype(bf16)), ms_t, gr

    nxt = prep(0)
    for c in range(nchunk):
        rows = slice(c * BLOCK, (c + 1) * BLOCK)
        q_b, kT_b, vT, ro_t, mq_b, mkT_b, mvT, mvT_bd, ms_t, (PT, cum_al, total) = nxt
        if c + 1 < nchunk:
            nxt = prep(c + 1)

        ro_t = ro_t + dqf_ref[...] * _dot_nt(R_ref[...].astype(bf16), q_b)
        ro_t = ro_t + dqb_ref[...] * _dot_nt(rb_ref[0, c], q_b)
        R_ref[...] = R_ref[...] * g_blk + jnp.where(bd, _dot_nt((vT * dk_ref[...]).astype(bf16), kT_b), 0.0)
        rg = rest_ref[0, rows, 3 * HW:4 * HW]
        ret = (rg * _sigmoid(rg)) * _head_layernorm_t(ro_t, rnw_ref[...]).T

        hf = _mlstm_direction(0, key_i <= qry_i, ms_t, PT, cum_al, M_ref[...], mvT_bd,
                              S_ref[...].astype(bf16), N_ref[...].astype(bf16), mq_b)
        hb = _mlstm_direction(8, key_i >= qry_i, ms_t, PT, cum_al, mb_ref[0, c], mvT_bd,
                              sb_ref[0, c], nb_ref[0, c], mq_b)
        mo = rest_ref[0, rows, 7 * HW:8 * HW]
        ml = _sigmoid(mo) * _head_layernorm_t(hf + hb, mnw_ref[...]).T
        _mlstm_state_step(0, PT, cum_al, total, mkT_b, mvT, S_ref, N_ref, M_ref, bd)

        o_ref[0, rows, :] = jnp.concatenate([ret, ml], axis=-1).astype(bf16)


def _halo_specs(tm, nt):
    r8 = tm // 8
    prev = lambda b, t: (b, jnp.maximum(t * r8 - 1, 0), 2)
    nxt = lambda b, t: (b, jnp.minimum((t + 1) * r8, nt * r8 - 1), 2)
    return prev, nxt


def _bwd_states(rest, gates, cw, gb, lgb, tm):
    B, Lp, _ = rest.shape
    nt = Lp // tm
    nchunk = tm // BLOCK
    rev = lambda f: (lambda b, i: f(b, nt - 1 - i))
    r8 = tm // 8
    col = lambda j: pl.BlockSpec((1, tm, HW), lambda b, i: (b, nt - 1 - i, j))
    const2 = lambda b, i: (0, 0)
    state_spec = lambda r, c: pl.BlockSpec((1, nchunk, r, c), lambda b, i: (b, nt - 1 - i, 0, 0))
    nc = Lp // BLOCK
    return pl.pallas_call(
        functools.partial(_bwd_state_kernel, nchunk=nchunk),
        grid=(B, nt),
        in_specs=[
            col(1), col(2), col(5), col(6),
            pl.BlockSpec((1, 8, HW), rev(lambda b, t: (b, jnp.maximum(t * r8 - 1, 0), 5))),
            pl.BlockSpec((1, 8, HW), rev(lambda b, t: (b, jnp.minimum((t + 1) * r8, nt * r8 - 1), 5))),
            pl.BlockSpec((1, tm, BLOCK), rev(lambda b, t: (b, t, 0))),
            pl.BlockSpec((3, 2 * HW), const2),
            pl.BlockSpec((1, BLOCK), const2),
            pl.BlockSpec((HW, BLOCK), const2),
        ],
        out_specs=[state_spec(HW, HW), state_spec(HW, HW), state_spec(16, HW), state_spec(16, BLOCK)],
        out_shape=[
            jax.ShapeDtypeStruct((B, nc, HW, HW), bf16),
            jax.ShapeDtypeStruct((B, nc, HW, HW), bf16),
            jax.ShapeDtypeStruct((B, nc, 16, HW), bf16),
            jax.ShapeDtypeStruct((B, nc, 16, BLOCK), f32),
        ],
        scratch_shapes=[
            pltpu.VMEM((HW, HW), f32), pltpu.VMEM((HW, HW), f32), pltpu.VMEM((16, HW), f32),
            pltpu.VMEM((16, BLOCK), f32), pltpu.VMEM((HW, BLOCK), f32),
        ],
        compiler_params=pltpu.CompilerParams(
            dimension_semantics=("parallel", "arbitrary"), vmem_limit_bytes=VMEM_LIMIT),
        name="bwd_states",
    )(rest, rest, rest, rest, rest, rest, gates, cw, gb, lgb)


def _fwd_sweep(rest, gates, rb, sb, nbw, mb, cw, gb, lgf, lgb, lgf4, lgb4, rnw, mnw, tm):
    B, Lp, _ = rest.shape
    nt = Lp // tm
    nchunk = tm // BLOCK
    prev, nxt = _halo_specs(tm, nt)
    const2 = lambda b, t: (0, 0)
    state_spec = lambda r, c: pl.BlockSpec((1, nchunk, r, c), lambda b, t: (b, t, 0, 0))
    return pl.pallas_call(
        functools.partial(_fwd_kernel, nchunk=nchunk),
        grid=(B, nt),
        in_specs=[
            pl.BlockSpec((1, tm, REST_W), lambda b, t: (b, t, 0)),
            pl.BlockSpec((1, 8, 2 * HW), prev),
            pl.BlockSpec((1, 8, 2 * HW), nxt),
            pl.BlockSpec((1, tm, BLOCK), lambda b, t: (b, t, 0)),
            state_spec(HW, HW), state_spec(HW, HW), state_spec(16, HW), state_spec(16, BLOCK),
            pl.BlockSpec((3, 2 * HW), const2),
            pl.BlockSpec((1, BLOCK), const2),
            pl.BlockSpec((HW, BLOCK), const2),
            pl.BlockSpec((HW, BLOCK), const2),
            pl.BlockSpec((4 * BLOCK, BLOCK), const2),
            pl.BlockSpec((4 * BLOCK, BLOCK), const2),
            pl.BlockSpec((HW, BLOCK), const2),
            pl.BlockSpec((HW, BLOCK), const2),
        ],
        out_specs=pl.BlockSpec((1, tm, 2 * HW), lambda b, t: (b, t, 0)),
        out_shape=jax.ShapeDtypeStruct((B, Lp, 2 * HW), bf16),
        scratch_shapes=[
            pltpu.VMEM((HW, HW), f32), pltpu.VMEM((HW, HW), f32), pltpu.VMEM((16, HW), f32),
            pltpu.VMEM((16, BLOCK), f32),
            pltpu.VMEM((HW, BLOCK), f32), pltpu.VMEM((HW, BLOCK), f32), pltpu.VMEM((HW, BLOCK), f32),
            pltpu.VMEM((4 * BLOCK, BLOCK), f32),
        ],
        compiler_params=pltpu.CompilerParams(
            dimension_semantics=("parallel", "arbitrary"), vmem_limit_bytes=VMEM_LIMIT),
        name="fwd_sweep",
    )(rest, rest, rest, gates, rb, sb, nbw, mb, cw, gb, lgf, lgb, lgf4, lgb4, rnw, mnw)


def _out_kernel(h_ref, a_ref, rm_ref, w_ref, o_ref):
    y = h_ref[0] + (_dot(a_ref[0], w_ref[0:ATT_W, :]) + _dot(rm_ref[0], w_ref[ATT_W:, :]))
    rows = lax.broadcasted_iota(jnp.int32, (y.shape[0], 1), 0)
    keep = jnp.logical_or(pl.program_id(1) > 0, rows >= PAD)
    o_ref[0] = jnp.where(keep, y, 0.0)


def _out_proj(h, att_o, rm, w_out, tm):
    B, Lp, _ = h.shape
    return pl.pallas_call(
        _out_kernel,
        grid=(B, Lp // tm),
        in_specs=[
            pl.BlockSpec((1, tm, D_MODEL), lambda b, i: (b, i, 0)),
            pl.BlockSpec((1, tm, ATT_W), lambda b, i: (b, i, 0)),
            pl.BlockSpec((1, tm, 2 * HW), lambda b, i: (b, i, 0)),
            pl.BlockSpec((D_MODEL, D_MODEL), lambda b, i: (0, 0)),
        ],
        out_specs=pl.BlockSpec((1, tm, D_MODEL), lambda b, i: (b, i, 0)),
        out_shape=jax.ShapeDtypeStruct(h.shape, f32),
        compiler_params=pltpu.CompilerParams(
            dimension_semantics=("parallel", "parallel"), vmem_limit_bytes=VMEM_LIMIT),
        name="out_proj",
    )(h, att_o, rm, w_out)


def _ffn_kernel(h_ref, hp_ref, hn_ref, n2_ref, wup_ref, cw_ref, wdn_ref, o_ref, act_ref):
    i = pl.program_id(1)
    last = i == pl.num_programs(1) - 1
    nw = n2_ref[...]

    def norm(x):
        ms = jnp.mean(x * x, axis=-1, keepdims=True)
        return ((x * lax.rsqrt(ms + EPS)) * nw).astype(bf16)

    x = h_ref[0]
    tm = x.shape[0]
    xe = norm(jnp.concatenate([x, hp_ref[0], jnp.where(last, 0.0, hn_ref[0])], axis=0))
    rows = lax.broadcasted_iota(jnp.int32, (tm, 1), 0)
    first_row = rows == 0
    last_row = rows == tm - 1

    def conv(ue, w3):
        u = ue[0:tm]
        up = jnp.where(first_row, ue[tm + 7:tm + 8, :], pltpu.roll(u, 1, 0))
        dn = jnp.where(last_row, ue[tm + 8:tm + 9, :], pltpu.roll(u, tm - 1, 0))
        return (up * w3[0:1, :] + u * w3[1:2, :]) + dn * w3[2:3, :]

    nck = D_FF // FF_CHUNK

    def up(c):
        gs = slice(c * FF_CHUNK, (c + 1) * FF_CHUNK)
        vs = slice(D_FF + c * FF_CHUNK, D_FF + (c + 1) * FF_CHUNK)
        return _dot(xe, wup_ref[:, gs]), _dot(xe, wup_ref[:, vs])

    u_next = up(0)
    acc = None
    done = 0
    for c in range(nck):
        ug, uv = u_next
        if c + 1 < nck:
            u_next = up(c + 1)
        gs = slice(c * FF_CHUNK, (c + 1) * FF_CHUNK)
        vs = slice(D_FF + c * FF_CHUNK, D_FF + (c + 1) * FF_CHUNK)
        g = conv(ug, cw_ref[:, gs])
        v = conv(uv, cw_ref[:, vs])
        act_ref[:, gs] = ((g * _sigmoid(g)) * v).astype(bf16)
        if (c + 1) % DOWN_GROUP == 0 or c + 1 == nck:
            ks = slice(done * FF_CHUNK, (c + 1) * FF_CHUNK)
            part = _dot(act_ref[:, ks], wdn_ref[ks, :])
            acc = part if acc is None else acc + part
            done = c + 1
    y = x + acc
    keep = jnp.logical_or(i > 0, rows >= PAD)
    o_ref[0] = jnp.where(keep, y, 0.0)


def _ffn(h, n2, w_up, cw, w_dn, tm):
    B, Lp, _ = h.shape
    nt = Lp // tm
    r8 = tm // 8
    return pl.pallas_call(
        _ffn_kernel,
        grid=(B, nt),
        in_specs=[
            pl.BlockSpec((1, tm, D_MODEL), lambda b, i: (b, i, 0)),
            pl.BlockSpec((1, 8, D_MODEL), lambda b, i: (b, jnp.maximum(i * r8 - 1, 0), 0)),
            pl.BlockSpec((1, 8, D_MODEL), lambda b, i: (b, jnp.minimum((i + 1) * r8, nt * r8 - 1), 0)),
            pl.BlockSpec((1, D_MODEL), lambda b, i: (0, 0)),
            pl.BlockSpec((D_MODEL, 2 * D_FF), lambda b, i: (0, 0), pipeline_mode=pl.Buffered(1)),
            pl.BlockSpec((3, 2 * D_FF), lambda b, i: (0, 0)),
            pl.BlockSpec((D_FF, D_MODEL), lambda b, i: (0, 0), pipeline_mode=pl.Buffered(1)),
        ],
        out_specs=pl.BlockSpec((1, tm, D_MODEL), lambda b, i: (b, i, 0)),
        out_shape=jax.ShapeDtypeStruct(h.shape, f32),
        scratch_shapes=[pltpu.VMEM((tm, D_FF), bf16)],
        compiler_params=pltpu.CompilerParams(
            dimension_semantics=("parallel", "parallel"), vmem_limit_bytes=VMEM_LIMIT),
        name="ffn",
    )(h, h, h, n2, w_up, cw, w_dn)


def _row_groups(x, height):
    col = jnp.repeat(x, height, axis=-1)
    return jnp.broadcast_to(col[:, :, None], col.shape + (BLOCK,))


def _lane_bcast(w):
    return jnp.broadcast_to(w.astype(f32)[:, :, None], w.shape + (BLOCK,))


def _prepare_params(norm1_w, w_in, attn_q_norm_w, attn_k_norm_w, attn_sink, ret_decay_logit,
                    ret_norm_w, mlstm_conv_w, mlstm_gate_b, mlstm_norm_w, w_out, norm2_w,
                    ffn_up, ffn_conv_w, ffn_down):
    scale = HEAD_DIM ** -0.5
    qrow = jnp.tile(attn_q_norm_w.astype(f32) * scale, (1, 2))
    krow = jnp.tile(attn_k_norm_w.astype(f32), (1, 2))
    qkw = jnp.concatenate([jnp.stack([qrow] * 4 + [krow], axis=1),
                           jnp.zeros((DEPTH, 3, BLOCK), f32)], axis=1)
    sink = jnp.repeat(attn_sink.astype(f32).reshape(DEPTH, ATT_KV_HEADS, 4), BLOCK, axis=-1)
    log_gamma = jax.nn.log_sigmoid(ret_decay_logit.astype(f32))
    gb = jnp.concatenate([mlstm_gate_b.astype(f32), jnp.zeros((DEPTH, BLOCK - N_GATES), f32)], axis=-1)
    return dict(
        n1=norm1_w.astype(f32)[:, None, :],
        w_main=w_in[:, :, :MAIN_W].astype(bf16),
        w_gate=jnp.concatenate([w_in[:, :, MAIN_W:], jnp.zeros((DEPTH, D_MODEL, BLOCK - N_GATES), w_in.dtype)],
                               axis=-1).astype(bf16),
        qkw=qkw,
        sink=sink[:, :, None, :],
        lgf=_row_groups(log_gamma[:, 0], HEAD_DIM),
        lgb=_row_groups(log_gamma[:, 1], HEAD_DIM),
        lgf4=_row_groups(log_gamma[:, 0], BLOCK),
        lgb4=_row_groups(log_gamma[:, 1], BLOCK),
        rnw=_lane_bcast(ret_norm_w),
        cw=mlstm_conv_w.astype(f32),
        gb=gb[:, None, :],
        mnw=_lane_bcast(mlstm_norm_w),
        w_out=w_out.astype(bf16),
        n2=norm2_w.astype(f32)[:, None, :],
        w_up=ffn_up.astype(bf16),
        fcw=ffn_conv_w.astype(f32),
        w_dn=ffn_down.astype(bf16),
    )


def _trunk(x, meta_tokens, params, bias):
    B, seq, _ = x.shape
    lp = seq + BLOCK
    tm = _row_tile(lp)
    meta = jnp.broadcast_to(meta_tokens[None].astype(x.dtype), (B, N_META, D_MODEL))
    h = jnp.concatenate([jnp.zeros((B, PAD, D_MODEL), x.dtype), meta, x], axis=1)

    def layer(h, p):
        att, rest, gates = _project(h, p["n1"], p["w_main"], p["w_gate"], p["qkw"], tm)
        att_o = _attention(att, bias, p["sink"], tm)
        rb, sb, nbw, mb = _bwd_states(rest, gates, p["cw"], p["gb"], p["lgb"], tm)
        rm = _fwd_sweep(rest, gates, rb, sb, nbw, mb, p["cw"], p["gb"], p["lgf"], p["lgb"],
                        p["lgf4"], p["lgb4"], p["rnw"], p["mnw"], tm)
        h = _out_proj(h, att_o, rm, p["w_out"], tm)
        h = _ffn(h, p["n2"], p["w_up"], p["fcw"], p["w_dn"], tm)
        return h, None

    h, _ = lax.scan(layer, h, params)
    return h[:, BLOCK:]


def kernel(x_prompt, x_sample, meta_tokens, norm1_w, w_in, attn_q_norm_w, attn_k_norm_w, attn_sink,
           ret_decay_logit, ret_norm_w, mlstm_conv_w, mlstm_gate_b, mlstm_norm_w, w_out, norm2_w,
           ffn_up, ffn_conv_w, ffn_down):
    params = _prepare_params(norm1_w, w_in, attn_q_norm_w, attn_k_norm_w, attn_sink, ret_decay_logit,
                             ret_norm_w, mlstm_conv_w, mlstm_gate_b, mlstm_norm_w, w_out, norm2_w,
                             ffn_up, ffn_conv_w, ffn_down)
    bias = _attention_bias()
    y_prompt = _trunk(x_prompt, meta_tokens, params, bias)
    y_sample = _trunk(x_sample, meta_tokens, params, bias)
    return (y_prompt, y_sample)
```

```python
import functools
import math

import jax
import jax.numpy as jnp
from jax import lax
from jax.experimental import pallas as pl
from jax.experimental.pallas import tpu as pltpu

f32 = jnp.float32
bf16 = jnp.bfloat16

D_MODEL = 1024
DEPTH = 4
N_META = 16
BLOCK = 128
PAD = BLOCK - N_META
HEAD_DIM = 64
ATT_HEADS = 8
ATT_KV_HEADS = 2
ATT_W = 512
KV_W = 128
HW = 256
N_GATES = 16
MAIN_W = 2816
REST_W = 2048
ATT_OUT_W = 1024
D_FF = 2816
FF_CHUNK = 256
DOWN_GROUP = 4
EPS = 1e-6
LOG2E = 1.4426950408889634
NEG = -1e30
VMEM_LIMIT = 56 * 1024 * 1024


def _row_tile(lp):
    nb = lp // BLOCK
    best = 1
    for t in range(1, 7):
        if nb % t == 0:
            best = t
    return best * BLOCK


def _sigmoid(x):
    return 1.0 / (1.0 + jnp.exp(-x))


def _log_sigmoid(x):
    return jnp.minimum(x, 0.0) - jnp.log(1.0 + jnp.exp(-jnp.abs(x)))


def _seg64_mean(x, lo):
    a = jnp.sum(jnp.where(lo, x, 0.0), axis=-1, keepdims=True)
    b = jnp.sum(jnp.where(lo, 0.0, x), axis=-1, keepdims=True)
    return jnp.where(lo, a, b) * (1.0 / HEAD_DIM)


def _block_stack(x_b):
    lane = lax.broadcasted_iota(jnp.int32, (1, HW), 1)
    parts = []
    for h in range(4):
        keep = (lane >= h * HEAD_DIM) & (lane < (h + 1) * HEAD_DIM)
        parts.append(jnp.where(keep, x_b, jnp.zeros_like(x_b)))
    return jnp.concatenate(parts, axis=0)


def _bd_mask():
    r = lax.broadcasted_iota(jnp.int32, (HW, HW), 0) >> 6
    c = lax.broadcasted_iota(jnp.int32, (HW, HW), 1) >> 6
    return r == c


def _dot(a, b):
    return jnp.dot(a, b, preferred_element_type=f32)


def _dot_nt(a, b):
    return lax.dot_general(a, b, (((1,), (1,)), ((), ())), preferred_element_type=f32)


def _split3(x):
    p1 = x.astype(bf16)
    r1 = x - p1.astype(f32)
    p2 = r1.astype(bf16)
    p3 = (r1 - p2.astype(f32)).astype(bf16)
    return p1, p2, p3


def _dot01_right(x, m01):
    p1, p2, p3 = _split3(x)
    return (_dot(p3, m01) + _dot(p2, m01)) + _dot(p1, m01)


def _proj_kernel(h_ref, n1_ref, w_ref, wg_ref, qkw_ref, att_ref, rest_ref, gate_ref):
    x = h_ref[0]
    ms = jnp.mean(x * x, axis=-1, keepdims=True)
    hn = ((x * lax.rsqrt(ms + EPS)) * n1_ref[...]).astype(bf16)
    pa = _dot(hn, w_ref[:, 0:ATT_W + 2 * KV_W])
    lo = lax.broadcasted_iota(jnp.int32, (1, BLOCK), 1) < HEAD_DIM
    pieces = []
    for c in range(5):
        xc = pa[:, c * BLOCK:(c + 1) * BLOCK]
        msq = _seg64_mean(xc * xc, lo)
        pieces.append((xc * lax.rsqrt(msq + EPS)) * qkw_ref[c:c + 1, :])
    k = pieces[4]
    v = pa[:, ATT_W + KV_W:ATT_W + 2 * KV_W]
    k_sw = pltpu.roll(k, HEAD_DIM, 1)
    v_sw = pltpu.roll(v, HEAD_DIM, 1)
    out = pieces[:4] + [jnp.where(lo, k, k_sw), jnp.where(lo, k_sw, k),
                        jnp.where(lo, v, v_sw), jnp.where(lo, v_sw, v)]
    att_ref[0] = jnp.concatenate(out, axis=-1).astype(bf16)
    rest_ref[0] = _dot(hn, w_ref[:, ATT_W + 2 * KV_W:MAIN_W])
    gate_ref[0] = _dot(hn, wg_ref[...])


def _project(h, n1, w_main, w_gate, qkw, tm):
    B, Lp, _ = h.shape
    grid = (B, Lp // tm)
    return pl.pallas_call(
        _proj_kernel,
        grid=grid,
        in_specs=[
            pl.BlockSpec((1, tm, D_MODEL), lambda b, i: (b, i, 0)),
            pl.BlockSpec((1, D_MODEL), lambda b, i: (0, 0)),
            pl.BlockSpec((D_MODEL, MAIN_W), lambda b, i: (0, 0)),
            pl.BlockSpec((D_MODEL, BLOCK), lambda b, i: (0, 0)),
            pl.BlockSpec((8, BLOCK), lambda b, i: (0, 0)),
        ],
        out_specs=[
            pl.BlockSpec((1, tm, ATT_OUT_W), lambda b, i: (b, i, 0)),
            pl.BlockSpec((1, tm, REST_W), lambda b, i: (b, i, 0)),
            pl.BlockSpec((1, tm, BLOCK), lambda b, i: (b, i, 0)),
        ],
        out_shape=[
            jax.ShapeDtypeStruct((B, Lp, ATT_OUT_W), bf16),
            jax.ShapeDtypeStruct((B, Lp, REST_W), f32),
            jax.ShapeDtypeStruct((B, Lp, BLOCK), f32),
        ],
        compiler_params=pltpu.CompilerParams(
            dimension_semantics=("parallel", "parallel"), vmem_limit_bytes=VMEM_LIMIT),
        name="proj",
    )(h, n1, w_main, w_gate, qkw)


def _attn_kernel(q_ref, kvp_ref, kvc_ref, kvn_ref, kvm_ref, bias_ref, sink_ref, o_ref, *, nblk):
    i = pl.program_id(1)
    nb = pl.num_programs(1) * nblk
    lo = lax.broadcasted_iota(jnp.int32, (1, BLOCK), 1) < HEAD_DIM
    top = lax.broadcasted_iota(jnp.int32, (BLOCK, 1), 0) < HEAD_DIM
    zpad = jnp.zeros((BLOCK - N_META, BLOCK), bf16)
    kvs = ([kvp_ref[0]] + [kvc_ref[0, u * BLOCK:(u + 1) * BLOCK, :] for u in range(nblk)] + [kvn_ref[0]])
    meta = kvm_ref[0]

    def v_t(blk, kh):
        return blk[:, (2 + kh) * BLOCK:(3 + kh) * BLOCK].astype(f32).T.astype(bf16)

    vts = [[v_t(blk, kh) for blk in kvs] for kh in range(ATT_KV_HEADS)]
    vt_meta = [v_t(jnp.concatenate([meta, jnp.zeros((BLOCK - N_META, 4 * BLOCK), bf16)], axis=0), kh)
               for kh in range(ATT_KV_HEADS)]

    def scores(t, kh):
        j = i * nblk + t
        var = jnp.where(j == 0, 1, jnp.where(j == 1, 2, jnp.where(j == nb - 1, 3, 0)))
        q = q_ref[0, t * BLOCK:(t + 1) * BLOCK, :]
        ksl = slice(kh * BLOCK, (kh + 1) * BLOCK)
        kcat = jnp.concatenate([kvs[t][:, ksl], kvs[t + 1][:, ksl], kvs[t + 2][:, ksl],
                                meta[:, ksl], zpad], axis=0)
        qparts = []
        for g in range(4):
            grp = q[:, (kh * 2 + g // 2) * BLOCK:(kh * 2 + g // 2 + 1) * BLOCK]
            keep = lo if g % 2 == 0 else jnp.logical_not(lo)
            qparts.append(jnp.where(keep, grp, jnp.zeros_like(grp)))
        qg = jnp.concatenate(qparts, axis=0)
        return _dot_nt(kcat, qg) + bias_ref[var * ATT_KV_HEADS + kh]

    units = [(t, kh) for t in range(nblk) for kh in range(ATT_KV_HEADS)]
    s_next = scores(*units[0])
    outs = []
    for n, (t, kh) in enumerate(units):
        s = s_next
        if n + 1 < len(units):
            s_next = scores(*units[n + 1])
        vt_cat = jnp.concatenate([vts[kh][t], vts[kh][t + 1], vts[kh][t + 2], vt_meta[kh]], axis=1)
        sk = sink_ref[kh]
        m = jnp.maximum(jnp.max(s, axis=0, keepdims=True), sk)
        p = jnp.exp2(s - m)
        den = jnp.sum(p, axis=0, keepdims=True) + jnp.exp2(sk - m)
        o_t = _dot(vt_cat, p.astype(bf16)) * (1.0 / den)
        for c in range(2):
            pair = jnp.where(top, o_t[:, (2 * c) * BLOCK:(2 * c + 1) * BLOCK],
                             o_t[:, (2 * c + 1) * BLOCK:(2 * c + 2) * BLOCK])
            outs.append(pair.T)
        if kh == ATT_KV_HEADS - 1:
            o_ref[0, t * BLOCK:(t + 1) * BLOCK, :] = jnp.concatenate(outs, axis=-1).astype(bf16)
            outs = []


def _attention(att, bias, sink, tm):
    B, Lp, _ = att.shape
    nblk = tm // BLOCK
    nt = Lp // tm
    assert Lp // BLOCK >= 4
    meta_row = PAD // N_META
    return pl.pallas_call(
        functools.partial(_attn_kernel, nblk=nblk),
        grid=(B, nt),
        in_specs=[
            pl.BlockSpec((1, tm, ATT_W), lambda b, i: (b, i, 0)),
            pl.BlockSpec((1, BLOCK, ATT_W), lambda b, i: (b, jnp.maximum(i * nblk - 1, 0), 1)),
            pl.BlockSpec((1, tm, ATT_W), lambda b, i: (b, i, 1)),
            pl.BlockSpec((1, BLOCK, ATT_W), lambda b, i: (b, jnp.minimum((i + 1) * nblk, nt * nblk - 1), 1)),
            pl.BlockSpec((1, N_META, ATT_W), lambda b, i: (b, meta_row, 1)),
            pl.BlockSpec((4 * ATT_KV_HEADS, 4 * BLOCK, 4 * BLOCK), lambda b, i: (0, 0, 0),
                         pipeline_mode=pl.Buffered(1)),
            pl.BlockSpec((ATT_KV_HEADS, 1, 4 * BLOCK), lambda b, i: (0, 0, 0)),
        ],
        out_specs=pl.BlockSpec((1, tm, ATT_W), lambda b, i: (b, i, 0)),
        out_shape=jax.ShapeDtypeStruct((B, Lp, ATT_W), bf16),
        compiler_params=pltpu.CompilerParams(
            dimension_semantics=("parallel", "parallel"), vmem_limit_bytes=VMEM_LIMIT),
        name="attn",
    )(att, att, att, att, att, bias, sink)


def _attention_bias():
    i = jnp.arange(BLOCK)[:, None]
    c = jnp.arange(4 * BLOCK)[None, :]
    dist = jnp.abs(i + BLOCK - c)
    band = (c < 3 * BLOCK) & (dist <= BLOCK)
    meta = (c >= 3 * BLOCK) & (c < 3 * BLOCK + N_META)
    slopes = jnp.exp2(-8.0 * jnp.arange(1, ATT_HEADS + 1, dtype=f32) / ATT_HEADS)
    blk = c // BLOCK
    variants = []
    for dead in ((), (0, 1), (0,), (2,)):
        ok = band
        for d in dead:
            ok = ok & (blk != d)
        b = jnp.where(ok[None], -(slopes[:, None, None] * dist[None].astype(f32)) * LOG2E,
                      jnp.where(meta[None], 0.0, NEG))
        variants.append(b.reshape(ATT_KV_HEADS, 4 * BLOCK, 4 * BLOCK).transpose(0, 2, 1))
    return jnp.concatenate(variants, axis=0).astype(f32)


def _conv3_silu(x, prev8, nxt8, w3):
    tm = x.shape[0]
    rows = lax.broadcasted_iota(jnp.int32, (tm, 1), 0)
    up = jnp.where(rows == 0, prev8[7:8, :], pltpu.roll(x, 1, 0))
    dn = jnp.where(rows == tm - 1, nxt8[0:1, :], pltpu.roll(x, tm - 1, 0))
    y = (up * w3[0:1, :] + x * w3[1:2, :]) + dn * w3[2:3, :]
    return y * _sigmoid(y)


def _rows_to_heads(rows):
    return jnp.concatenate([jnp.broadcast_to(r, (HEAD_DIM, r.shape[1])) for r in rows], axis=0)


def _bd_lanes(xT_b):
    row_head = lax.broadcasted_iota(jnp.int32, (HW, 1), 0) >> 6
    return jnp.concatenate([jnp.where(row_head == h, xT_b, jnp.zeros_like(xT_b)) for h in range(4)], axis=1)


def _gate_rows(gates, gbias):
    gt = (gates + gbias).T[0:16, :]
    grow = lax.broadcasted_iota(jnp.int32, (16, 1), 0)
    PT = jnp.where((grow & 4) != 0, _log_sigmoid(gt), gt)
    r = lax.broadcasted_iota(jnp.int32, (BLOCK, 2 * BLOCK), 0)
    c = lax.broadcasted_iota(jnp.int32, (BLOCK, 2 * BLOCK), 1)
    tri = ((c < BLOCK) & (r <= c)) | ((c >= BLOCK) & (r >= c - BLOCK))
    cum2 = _dot01_right(PT, jnp.where(tri, 1.0, 0.0).astype(bf16))
    row = lax.broadcasted_iota(jnp.int32, (16, 1), 0)
    cum = jnp.where(row < 8, cum2[:, 0:BLOCK], cum2[:, BLOCK:2 * BLOCK])
    cum_al = pltpu.roll(cum, 12, 0)
    total = jnp.where(row < 8, cum_al[:, BLOCK - 1:BLOCK], cum_al[:, 0:1])
    return PT, cum_al, total


def _mlstm_state_step(r0, PT, cum_al, total, kT_b, vT, S_ref, N_ref, M_ref, bd):
    wend = (total - cum_al) + PT
    m_loc = jnp.max(wend, axis=-1, keepdims=True)
    e_end = jnp.exp(wend - m_loc)
    m_old = M_ref[:, 0:1]
    m_new = jnp.maximum(total + m_old, m_loc)
    a_rep = jnp.broadcast_to(jnp.exp((total + m_old) - m_new), (16, BLOCK))
    c_rep = jnp.broadcast_to(jnp.exp(m_loc - m_new), (16, BLOCK))
    M_ref[...] = jnp.broadcast_to(m_new, (16, BLOCK))
    e_heads = _rows_to_heads([e_end[r0 + h:r0 + h + 1, :] for h in range(4)])
    upd = _dot_nt((vT * e_heads).astype(bf16), kT_b)
    a_m = _rows_to_heads([a_rep[r0 + h:r0 + h + 1, :] for h in range(4)])
    c_m = _rows_to_heads([c_rep[r0 + h:r0 + h + 1, :] for h in range(4)])
    a_m2 = jnp.concatenate([a_m, a_m], axis=1)
    c_m2 = jnp.concatenate([c_m, c_m], axis=1)
    S_ref[...] = a_m2 * S_ref[...] + c_m2 * jnp.where(bd, upd, 0.0)
    n_loc = _dot_nt(e_end.astype(bf16), kT_b)
    lane_head = lax.broadcasted_iota(jnp.int32, (8, HW), 1) >> 6
    row8 = lax.broadcasted_iota(jnp.int32, (8, HW), 0)
    a8 = jnp.concatenate([a_rep[r0:r0 + 8, :], a_rep[r0:r0 + 8, :]], axis=1)
    c8 = jnp.concatenate([c_rep[r0:r0 + 8, :], c_rep[r0:r0 + 8, :]], axis=1)
    N_ref[0:8, :] = a8 * N_ref[0:8, :] + c8 * jnp.where(lane_head == row8, n_loc[r0:r0 + 8, :], 0.0)


def _mlstm_inputs(rest_ref, prev_ref, next_ref, cw_ref, first, last):
    x = rest_ref[0, :, 4 * HW:6 * HW]
    tm = x.shape[0]
    prev8 = jnp.where(first, 0.0, prev_ref[0])
    nxt8 = jnp.where(last, 0.0, next_ref[0])
    y = _conv3_silu(x, prev8, nxt8, cw_ref[...])
    rows = lax.broadcasted_iota(jnp.int32, (tm, 1), 0)
    keep = jnp.logical_or(jnp.logical_not(first), rows >= PAD)
    mq = y[:, 0:HW]
    mk = jnp.where(keep, y[:, HW:2 * HW], 0.0) * (HEAD_DIM ** -0.5)
    return mq, mk


def _bwd_state_kernel(rk_ref, rv_ref, mkx_ref, mv_ref, prev_ref, next_ref, gate_ref, cw_ref, gb_ref, lgb_ref,
                      r_out, s_out, n_out, m_out,
                      R_ref, S_ref, N_ref, M_ref, dk_ref, *, nchunk):
    i = pl.program_id(1)
    nt = pl.num_programs(1)
    ti = nt - 1 - i

    @pl.when(i == 0)
    def _():
        R_ref[...] = jnp.zeros_like(R_ref)
        S_ref[...] = jnp.zeros_like(S_ref)
        N_ref[...] = jnp.zeros_like(N_ref)
        M_ref[...] = jnp.zeros_like(M_ref)
        pos = lax.broadcasted_iota(jnp.int32, (1, BLOCK), 1).astype(f32)
        dk_ref[...] = jnp.exp(lgb_ref[...] * pos)

    bd = _bd_mask()
    g1 = jnp.exp(lgb_ref[...] * float(BLOCK))
    g_blk = jnp.concatenate([g1, g1], axis=1)
    first = ti == 0
    y = _conv3_silu(mkx_ref[0], jnp.where(first, 0.0, prev_ref[0]), jnp.where(ti == nt - 1, 0.0, next_ref[0]),
                    cw_ref[:, HW:2 * HW])
    trow = lax.broadcasted_iota(jnp.int32, (y.shape[0], 1), 0)
    mk_all = jnp.where(jnp.logical_or(jnp.logical_not(first), trow >= PAD), y, 0.0) * (HEAD_DIM ** -0.5)

    def prep(c):
        rows = slice(c * BLOCK, (c + 1) * BLOCK)
        kT_b = (rk_ref[0, rows, :] * (HEAD_DIM ** -0.5)).T.astype(bf16)
        vT = rv_ref[0, rows, :].T
        mkT_b = mk_all[rows].T.astype(bf16)
        mvT = mv_ref[0, rows, :].T
        return kT_b, vT, mkT_b, mvT, _gate_rows(gate_ref[0, rows, :], gb_ref[...])

    order = list(reversed(range(nchunk)))
    nxt = prep(order[0])
    for n, c in enumerate(order):
        kT_b, vT, mkT_b, mvT, (PT, cum_al, total) = nxt
        if n + 1 < nchunk:
            nxt = prep(order[n + 1])
        r_out[0, c] = R_ref[...].astype(bf16)
        s_out[0, c] = S_ref[...].astype(bf16)
        n_out[0, c] = N_ref[...].astype(bf16)
        m_out[0, c] = M_ref[...]
        R_ref[...] = R_ref[...] * g_blk + jnp.where(bd, _dot_nt((vT * dk_ref[...]).astype(bf16), kT_b), 0.0)
        _mlstm_state_step(8, PT, cum_al, total, mkT_b, mvT, S_ref, N_ref, M_ref, bd)


def _head_layernorm_t(x, w_t):
    outs = []
    for h in range(4):
        xh = x[h * HEAD_DIM:(h + 1) * HEAD_DIM, :]
        mu = jnp.mean(xh, axis=0, keepdims=True)
        d = xh - mu
        var = jnp.mean(d * d, axis=0, keepdims=True)
        outs.append(d * lax.rsqrt(var + EPS))
    return jnp.concatenate(outs, axis=0) * w_t


def _mlstm_direction(r0, mask, s_t, PT, cum_al, m_prev, vT_bd, S_b, N_b, q_b):
    qn = _dot_nt(N_b, q_b)
    ws, a_rows, r_rows = [], [], []
    for h in range(4):
        rr = r0 + h
        b_row = cum_al[rr:rr + 1, :]
        cb = jnp.broadcast_to(PT[rr:rr + 1, :] - b_row, (BLOCK, BLOCK)).T
        dl = jnp.where(mask, b_row + cb, NEG)
        m_intra = jnp.max(dl, axis=0, keepdims=True)
        inter = b_row + m_prev[rr:rr + 1, :]
        m_t = jnp.maximum(m_intra, inter)
        a_i = jnp.exp(inter - m_t)
        w = jnp.exp(dl - m_t) * s_t[h * BLOCK:(h + 1) * BLOCK, :]
        den = jnp.sum(w, axis=0, keepdims=True) + a_i * qn[h:h + 1, :]
        r_rows.append(1.0 / jnp.maximum(jnp.abs(den), jnp.exp(-m_t)))
        a_rows.append(a_i)
        ws.append(w.astype(bf16))
    num = _dot(vT_bd, jnp.concatenate(ws, axis=0)) + _rows_to_heads(a_rows) * _dot_nt(S_b, q_b)
    return num * _rows_to_heads(r_rows)


def _fwd_kernel(rest_ref, prev_ref, next_ref, gate_ref, rb_ref, sb_ref, nb_ref, mb_ref,
                cw_ref, gb_ref, lgf_ref, lgb_ref, lgf4_ref, lgb4_ref, rnw_ref, mnw_ref,
                h_ref, ao_ref, wout_ref,
                o_ref,
                R_ref, S_ref, N_ref, M_ref, dk_ref, dqf_ref, dqb_ref, dcomb_ref, mix_ref,
                *, nchunk):
    i = pl.program_id(1)
    nt = pl.num_programs(1) - 1
    slot = lax.rem(i, 2)

    @pl.when(i == 0)
    def _():
        mix_ref[...] = jnp.zeros_like(mix_ref)
        R_ref[...] = jnp.zeros_like(R_ref)
        S_ref[...] = jnp.zeros_like(S_ref)
        N_ref[...] = jnp.zeros_like(N_ref)
        M_ref[...] = jnp.zeros_like(M_ref)
        pos = lax.broadcasted_iota(jnp.int32, (1, BLOCK), 1).astype(f32)
        dk_ref[...] = jnp.exp(lgf_ref[...] * ((BLOCK - 1.0) - pos))
        dqf_ref[...] = jnp.exp(lgf_ref[...] * (pos + 1.0))
        dqb_ref[...] = jnp.exp(lgb_ref[...] * (float(BLOCK) - pos))
        key = lax.broadcasted_iota(jnp.int32, (4 * BLOCK, BLOCK), 0) & (BLOCK - 1)
        qry = lax.broadcasted_iota(jnp.int32, (4 * BLOCK, BLOCK), 1)
        d = (qry - key).astype(f32)
        dcomb_ref[...] = (jnp.where(d >= 0, jnp.exp(lgf4_ref[...] * jnp.maximum(d, 0.0)), 0.0)
                          + jnp.where(d <= 0, jnp.exp(lgb4_ref[...] * jnp.maximum(-d, 0.0)), 0.0))

    bd = _bd_mask()
    g1 = jnp.exp(lgf_ref[...] * float(BLOCK))
    g_blk = jnp.concatenate([g1, g1], axis=1)
    key_i = lax.broadcasted_iota(jnp.int32, (BLOCK, BLOCK), 0)
    qry_i = lax.broadcasted_iota(jnp.int32, (BLOCK, BLOCK), 1)
    mq_all, mk_all = _mlstm_inputs(rest_ref, prev_ref, next_ref, cw_ref, i == 0, i >= nt - 1)

    x_att = ao_ref[0]
    x_mix = mix_ref[1 - slot]
    trow = lax.broadcasted_iota(jnp.int32, (x_att.shape[0], 1), 0)
    keep_rows = jnp.logical_or(i > 1, trow >= PAD)

    def finish_prev(j):
        cols = slice(j * HW, (j + 1) * HW)
        y = h_ref[0, :, cols] + (_dot(x_att, wout_ref[0:ATT_W, cols]) + _dot(x_mix, wout_ref[ATT_W:, cols]))
        o_ref[0, :, cols] = jnp.where(keep_rows, y, 0.0)

    n_piece = D_MODEL // HW

    def prep(c):
        rows = slice(c * BLOCK, (c + 1) * BLOCK)
        q_b = rest_ref[0, rows, 0:HW].astype(bf16)
        k = rest_ref[0, rows, HW:2 * HW] * (HEAD_DIM ** -0.5)
        k_b = k.astype(bf16)
        kT_b = k.T.astype(bf16)
        vT = rest_ref[0, rows, 2 * HW:3 * HW].T
        a_t = (_dot_nt(_block_stack(k_b), q_b) * dcomb_ref[...]).astype(bf16)
        ro_t = _dot(_bd_lanes(vT.astype(bf16)), a_t)
        mq_b = mq_all[rows].astype(bf16)
        mk = mk_all[rows]
        mkT_b = mk.T.astype(bf16)
        mvT = rest_ref[0, rows, 6 * HW:7 * HW].T
        ms_t = _dot_nt(_block_stack(mk.astype(bf16)), mq_b)
        gr = _gate_rows(gate_ref[0, rows, :], gb_ref[...])
        return q_b, kT_b, vT, ro_t, mq_b, mkT_b, mvT, _bd_lanes(mvT.astype(bf16)), ms_t, gr

    nxt = prep(0)
    for c in range(nchunk):
        rows = slice(c * BLOCK, (c + 1) * BLOCK)
        q_b, kT_b, vT, ro_t, mq_b, mkT_b, mvT, mvT_bd, ms_t, (PT, cum_al, total) = nxt
        if c + 1 < nchunk:
            nxt = prep(c + 1)
        for j in range(c * n_piece // nchunk, (c + 1) * n_piece // nchunk):
            finish_prev(j)

        ro_t = ro_t + dqf_ref[...] * _dot_nt(R_ref[...].astype(bf16), q_b)
        ro_t = ro_t + dqb_ref[...] * _dot_nt(rb_ref[0, c], q_b)
        R_ref[...] = R_ref[...] * g_blk + jnp.where(bd, _dot_nt((vT * dk_ref[...]).astype(bf16), kT_b), 0.0)
        rg = rest_ref[0, rows, 3 * HW:4 * HW]
        ret = (rg * _sigmoid(rg)) * _head_layernorm_t(ro_t, rnw_ref[...]).T

        hf = _mlstm_direction(0, key_i <= qry_i, ms_t, PT, cum_al, M_ref[...], mvT_bd,
                              S_ref[...].astype(bf16), N_ref[...].astype(bf16), mq_b)
        hb = _mlstm_direction(8, key_i >= qry_i, ms_t, PT, cum_al, mb_ref[0, c], mvT_bd,
                              sb_ref[0, c], nb_ref[0, c], mq_b)
        mo = rest_ref[0, rows, 7 * HW:8 * HW]
        ml = _sigmoid(mo) * _head_layernorm_t(hf + hb, mnw_ref[...]).T
        _mlstm_state_step(0, PT, cum_al, total, mkT_b, mvT, S_ref, N_ref, M_ref, bd)

        mix_ref[slot, rows, :] = jnp.concatenate([ret, ml], axis=-1).astype(bf16)


def _bwd_states(rest, gates, cw, gb, lgb, tm):
    B, Lp, _ = rest.shape
    nt = Lp // tm
    nchunk = tm // BLOCK
    rev = lambda f: (lambda b, i: f(b, nt - 1 - i))
    r8 = tm // 8
    col = lambda j: pl.BlockSpec((1, tm, HW), lambda b, i: (b, nt - 1 - i, j))
    const2 = lambda b, i: (0, 0)
    state_spec = lambda r, c: pl.BlockSpec((1, nchunk, r, c), lambda b, i: (b, nt - 1 - i, 0, 0))
    nc = Lp // BLOCK
    return pl.pallas_call(
        functools.partial(_bwd_state_kernel, nchunk=nchunk),
        grid=(B, nt),
        in_specs=[
            col(1), col(2), col(5), col(6),
            pl.BlockSpec((1, 8, HW), rev(lambda b, t: (b, jnp.maximum(t * r8 - 1, 0), 5))),
            pl.BlockSpec((1, 8, HW), rev(lambda b, t: (b, jnp.minimum((t + 1) * r8, nt * r8 - 1), 5))),
            pl.BlockSpec((1, tm, BLOCK), rev(lambda b, t: (b, t, 0))),
            pl.BlockSpec((3, 2 * HW), const2),
            pl.BlockSpec((1, BLOCK), const2),
            pl.BlockSpec((HW, BLOCK), const2),
        ],
        out_specs=[state_spec(HW, HW), state_spec(HW, HW), state_spec(16, HW), state_spec(16, BLOCK)],
        out_shape=[
            jax.ShapeDtypeStruct((B, nc, HW, HW), bf16),
            jax.ShapeDtypeStruct((B, nc, HW, HW), bf16),
            jax.ShapeDtypeStruct((B, nc, 16, HW), bf16),
            jax.ShapeDtypeStruct((B, nc, 16, BLOCK), f32),
        ],
        scratch_shapes=[
            pltpu.VMEM((HW, HW), f32), pltpu.VMEM((HW, HW), f32), pltpu.VMEM((16, HW), f32),
            pltpu.VMEM((16, BLOCK), f32), pltpu.VMEM((HW, BLOCK), f32),
        ],
        compiler_params=pltpu.CompilerParams(
            dimension_semantics=("parallel", "arbitrary"), vmem_limit_bytes=VMEM_LIMIT),
        name="bwd_states",
    )(rest, rest, rest, rest, rest, rest, gates, cw, gb, lgb)


def _fwd_sweep(rest, gates, rb, sb, nbw, mb, cw, gb, lgf, lgb, lgf4, lgb4, rnw, mnw, h, att_o, w_out, tm):
    B, Lp, _ = rest.shape
    nt = Lp // tm
    nchunk = tm // BLOCK
    r8 = tm // 8
    cur = lambda t: jnp.minimum(t, nt - 1)
    done = lambda t: jnp.maximum(t - 1, 0)
    const2 = lambda b, t: (0, 0)
    state_spec = lambda r, c: pl.BlockSpec((1, nchunk, r, c), lambda b, t: (b, cur(t), 0, 0))
    return pl.pallas_call(
        functools.partial(_fwd_kernel, nchunk=nchunk),
        grid=(B, nt + 1),
        in_specs=[
            pl.BlockSpec((1, tm, REST_W), lambda b, t: (b, cur(t), 0)),
            pl.BlockSpec((1, 8, 2 * HW), lambda b, t: (b, jnp.maximum(cur(t) * r8 - 1, 0), 2)),
            pl.BlockSpec((1, 8, 2 * HW), lambda b, t: (b, jnp.minimum((cur(t) + 1) * r8, nt * r8 - 1), 2)),
            pl.BlockSpec((1, tm, BLOCK), lambda b, t: (b, cur(t), 0)),
            state_spec(HW, HW), state_spec(HW, HW), state_spec(16, HW), state_spec(16, BLOCK),
            pl.BlockSpec((3, 2 * HW), const2),
            pl.BlockSpec((1, BLOCK), const2),
            pl.BlockSpec((HW, BLOCK), const2),
            pl.BlockSpec((HW, BLOCK), const2),
            pl.BlockSpec((4 * BLOCK, BLOCK), const2),
            pl.BlockSpec((4 * BLOCK, BLOCK), const2),
            pl.BlockSpec((HW, BLOCK), const2),
            pl.BlockSpec((HW, BLOCK), const2),
            pl.BlockSpec((1, tm, D_MODEL), lambda b, t: (b, done(t), 0)),
            pl.BlockSpec((1, tm, ATT_W), lambda b, t: (b, done(t), 0)),
            pl.BlockSpec((D_MODEL, D_MODEL), const2),
        ],
        out_specs=pl.BlockSpec((1, tm, D_MODEL), lambda b, t: (b, done(t), 0)),
        out_shape=jax.ShapeDtypeStruct((B, Lp, D_MODEL), f32),
        scratch_shapes=[
            pltpu.VMEM((HW, HW), f32), pltpu.VMEM((HW, HW), f32), pltpu.VMEM((16, HW), f32),
            pltpu.VMEM((16, BLOCK), f32),
            pltpu.VMEM((HW, BLOCK), f32), pltpu.VMEM((HW, BLOCK), f32), pltpu.VMEM((HW, BLOCK), f32),
            pltpu.VMEM((4 * BLOCK, BLOCK), f32),
            pltpu.VMEM((2, tm, 2 * HW), bf16),
        ],
        compiler_params=pltpu.CompilerParams(
            dimension_semantics=("parallel", "arbitrary"), vmem_limit_bytes=VMEM_LIMIT),
        name="fwd_sweep",
    )(rest, rest, rest, gates, rb, sb, nbw, mb, cw, gb, lgf, lgb, lgf4, lgb4, rnw, mnw, h, att_o, w_out)


def _ffn_kernel(h_ref, hp_ref, hn_ref, n2_ref, wup_ref, cw_ref, wdn_ref, o_ref, act_ref):
    i = pl.program_id(1)
    last = i == pl.num_programs(1) - 1
    nw = n2_ref[...]

    def norm(x):
        ms = jnp.mean(x * x, axis=-1, keepdims=True)
        return ((x * lax.rsqrt(ms + EPS)) * nw).astype(bf16)

    x = h_ref[0]
    tm = x.shape[0]
    xe = norm(jnp.concatenate([x, hp_ref[0], jnp.where(last, 0.0, hn_ref[0])], axis=0))
    rows = lax.broadcasted_iota(jnp.int32, (tm, 1), 0)
    first_row = rows == 0
    last_row = rows == tm - 1

    def conv(ue, w3):
        u = ue[0:tm]
        up = jnp.where(first_row, ue[tm + 7:tm + 8, :], pltpu.roll(u, 1, 0))
        dn = jnp.where(last_row, ue[tm + 8:tm + 9, :], pltpu.roll(u, tm - 1, 0))
        return (up * w3[0:1, :] + u * w3[1:2, :]) + dn * w3[2:3, :]

    nck = D_FF // FF_CHUNK

    def up(c):
        gs = slice(c * FF_CHUNK, (c + 1) * FF_CHUNK)
        vs = slice(D_FF + c * FF_CHUNK, D_FF + (c + 1) * FF_CHUNK)
        return _dot(xe, wup_ref[:, gs]), _dot(xe, wup_ref[:, vs])

    u_next = up(0)
    acc = None
    done = 0
    for c in range(nck):
        ug, uv = u_next
        if c + 1 < nck:
            u_next = up(c + 1)
        gs = slice(c * FF_CHUNK, (c + 1) * FF_CHUNK)
        vs = slice(D_FF + c * FF_CHUNK, D_FF + (c + 1) * FF_CHUNK)
        g = conv(ug, cw_ref[:, gs])
        v = conv(uv, cw_ref[:, vs])
        act_ref[:, gs] = ((g * _sigmoid(g)) * v).astype(bf16)
        if (c + 1) % DOWN_GROUP == 0 or c + 1 == nck:
            ks = slice(done * FF_CHUNK, (c + 1) * FF_CHUNK)
            part = _dot(act_ref[:, ks], wdn_ref[ks, :])
            acc = part if acc is None else acc + part
            done = c + 1
    y = x + acc
    keep = jnp.logical_or(i > 0, rows >= PAD)
    o_ref[0] = jnp.where(keep, y, 0.0)


def _ffn(h, n2, w_up, cw, w_dn, tm):
    B, Lp, _ = h.shape
    nt = Lp // tm
    r8 = tm // 8
    return pl.pallas_call(
        _ffn_kernel,
        grid=(B, nt),
        in_specs=[
            pl.BlockSpec((1, tm, D_MODEL), lambda b, i: (b, i, 0)),
            pl.BlockSpec((1, 8, D_MODEL), lambda b, i: (b, jnp.maximum(i * r8 - 1, 0), 0)),
            pl.BlockSpec((1, 8, D_MODEL), lambda b, i: (b, jnp.minimum((i + 1) * r8, nt * r8 - 1), 0)),
            pl.BlockSpec((1, D_MODEL), lambda b, i: (0, 0)),
            pl.BlockSpec((D_MODEL, 2 * D_FF), lambda b, i: (0, 0), pipeline_mode=pl.Buffered(1)),
            pl.BlockSpec((3, 2 * D_FF), lambda b, i: (0, 0)),
            pl.BlockSpec((D_FF, D_MODEL), lambda b, i: (0, 0), pipeline_mode=pl.Buffered(1)),
        ],
        out_specs=pl.BlockSpec((1, tm, D_MODEL), lambda b, i: (b, i, 0)),
        out_shape=jax.ShapeDtypeStruct(h.shape, f32),
        scratch_shapes=[pltpu.VMEM((tm, D_FF), bf16)],
        compiler_params=pltpu.CompilerParams(
            dimension_semantics=("parallel", "parallel"), vmem_limit_bytes=VMEM_LIMIT),
        name="ffn",
    )(h, h, h, n2, w_up, cw, w_dn)


def _row_groups(x, height):
    col = jnp.repeat(x, height, axis=-1)
    return jnp.broadcast_to(col[:, :, None], col.shape + (BLOCK,))


def _lane_bcast(w):
    return jnp.broadcast_to(w.astype(f32)[:, :, None], w.shape + (BLOCK,))


def _prepare_params(norm1_w, w_in, attn_q_norm_w, attn_k_norm_w, attn_sink, ret_decay_logit,
                    ret_norm_w, mlstm_conv_w, mlstm_gate_b, mlstm_norm_w, w_out, norm2_w,
                    ffn_up, ffn_conv_w, ffn_down):
    scale = HEAD_DIM ** -0.5
    qrow = jnp.tile(attn_q_norm_w.astype(f32) * (scale * LOG2E), (1, 2))
    krow = jnp.tile(attn_k_norm_w.astype(f32), (1, 2))
    qkw = jnp.concatenate([jnp.stack([qrow] * 4 + [krow], axis=1),
                           jnp.zeros((DEPTH, 3, BLOCK), f32)], axis=1)
    sink = jnp.repeat(attn_sink.astype(f32).reshape(DEPTH, ATT_KV_HEADS, 4) * LOG2E, BLOCK, axis=-1)
    log_gamma = jax.nn.log_sigmoid(ret_decay_logit.astype(f32))
    gb = jnp.concatenate([mlstm_gate_b.astype(f32), jnp.zeros((DEPTH, BLOCK - N_GATES), f32)], axis=-1)
    return dict(
        n1=norm1_w.astype(f32)[:, None, :],
        w_main=w_in[:, :, :MAIN_W].astype(bf16),
        w_gate=jnp.concatenate([w_in[:, :, MAIN_W:], jnp.zeros((DEPTH, D_MODEL, BLOCK - N_GATES), w_in.dtype)],
                               axis=-1).astype(bf16),
        qkw=qkw,
        sink=sink[:, :, None, :],
        lgf=_row_groups(log_gamma[:, 0], HEAD_DIM),
        lgb=_row_groups(log_gamma[:, 1], HEAD_DIM),
        lgf4=_row_groups(log_gamma[:, 0], BLOCK),
        lgb4=_row_groups(log_gamma[:, 1], BLOCK),
        rnw=_lane_bcast(ret_norm_w),
        cw=mlstm_conv_w.astype(f32),
        gb=gb[:, None, :],
        mnw=_lane_bcast(mlstm_norm_w),
        w_out=w_out.astype(bf16),
        n2=norm2_w.astype(f32)[:, None, :],
        w_up=ffn_up.astype(bf16),
        fcw=ffn_conv_w.astype(f32),
        w_dn=ffn_down.astype(bf16),
    )


def _trunk(x, meta_tokens, params, bias):
    B, seq, _ = x.shape
    lp = seq + BLOCK
    tm = _row_tile(lp)
    meta = jnp.broadcast_to(meta_tokens[None].astype(x.dtype), (B, N_META, D_MODEL))
    h = jnp.concatenate([jnp.zeros((B, PAD, D_MODEL), x.dtype), meta, x], axis=1)

    def layer(h, p):
        att, rest, gates = _project(h, p["n1"], p["w_main"], p["w_gate"], p["qkw"], tm)
        att_o = _attention(att, bias, p["sink"], tm)
        rb, sb, nbw, mb = _bwd_states(rest, gates, p["cw"], p["gb"], p["lgb"], tm)
        h = _fwd_sweep(rest, gates, rb, sb, nbw, mb, p["cw"], p["gb"], p["lgf"], p["lgb"],
                       p["lgf4"], p["lgb4"], p["rnw"], p["mnw"], h, att_o, p["w_out"], tm)
        h = _ffn(h, p["n2"], p["w_up"], p["fcw"], p["w_dn"], tm)
        return h, None

    h, _ = lax.scan(layer, h, params)
    return h[:, BLOCK:]


def kernel(x_prompt, x_sample, meta_tokens, norm1_w, w_in, attn_q_norm_w, attn_k_norm_w, attn_sink,
           ret_decay_logit, ret_norm_w, mlstm_conv_w, mlstm_gate_b, mlstm_norm_w, w_out, norm2_w,
           ffn_up, ffn_conv_w, ffn_down):
    params = _prepare_params(norm1_w, w_in, attn_q_norm_w, attn_k_norm_w, attn_sink, ret_decay_logit,
                             ret_norm_w, mlstm_conv_w, mlstm_gate_b, mlstm_norm_w, w_out, norm2_w,
                             ffn_up, ffn_conv_w, ffn_down)
    bias = _attention_bias()
    y_prompt = _trunk(x_prompt, meta_tokens, params, bias)
    y_sample = _trunk(x_sample, meta_tokens, params, bias)
    return (y_prompt, y_sample)
```

```python
import functools
import math

import jax
import jax.numpy as jnp
from jax import lax
from jax.experimental import pallas as pl
from jax.experimental.pallas import tpu as pltpu

f32 = jnp.float32
bf16 = jnp.bfloat16

D_MODEL = 1024
DEPTH = 4
N_META = 16
BLOCK = 128
PAD = BLOCK - N_META
HEAD_DIM = 64
ATT_HEADS = 8
ATT_KV_HEADS = 2
ATT_W = 512
KV_W = 128
HW = 256
N_GATES = 16
MAIN_W = 2816
REST_W = 2048
N_KEYS = 3 * BLOCK + N_META
ATT_OUT_W = 1024
D_FF = 2816
FF_CHUNK = 256
DOWN_GROUP = 11
EPS = 1e-6
LOG2E = 1.4426950408889634
NEG = -1e30
VMEM_LIMIT = 56 * 1024 * 1024


def _row_tile(lp):
    nb = lp // BLOCK
    best = 1
    for t in range(1, 7):
        if nb % t == 0:
            best = t
    return best * BLOCK


def _sigmoid(x):
    return 1.0 / (1.0 + jnp.exp(-x))


def _log_sigmoid(x):
    return jnp.minimum(x, 0.0) - jnp.log(1.0 + jnp.exp(-jnp.abs(x)))


def _seg64_mean(x, lo):
    a = jnp.sum(jnp.where(lo, x, 0.0), axis=-1, keepdims=True)
    b = jnp.sum(jnp.where(lo, 0.0, x), axis=-1, keepdims=True)
    return jnp.where(lo, a, b) * (1.0 / HEAD_DIM)


def _block_stack(x_b):
    lane = lax.broadcasted_iota(jnp.int32, (1, HW), 1)
    parts = []
    for h in range(4):
        keep = (lane >= h * HEAD_DIM) & (lane < (h + 1) * HEAD_DIM)
        parts.append(jnp.where(keep, x_b, jnp.zeros_like(x_b)))
    return jnp.concatenate(parts, axis=0)


def _bd_mask():
    r = lax.broadcasted_iota(jnp.int32, (HW, HW), 0) >> 6
    c = lax.broadcasted_iota(jnp.int32, (HW, HW), 1) >> 6
    return r == c


def _dot(a, b):
    return jnp.dot(a, b, preferred_element_type=f32)


def _dot_nt(a, b):
    return lax.dot_general(a, b, (((1,), (1,)), ((), ())), preferred_element_type=f32)


def _split3(x):
    p1 = x.astype(bf16)
    r1 = x - p1.astype(f32)
    p2 = r1.astype(bf16)
    p3 = (r1 - p2.astype(f32)).astype(bf16)
    return p1, p2, p3


def _dot01_right(x, m01):
    p1, p2, p3 = _split3(x)
    return (_dot(p3, m01) + _dot(p2, m01)) + _dot(p1, m01)


def _proj_kernel(h_ref, n1_ref, w_ref, wg_ref, qkw_ref, att_ref, rest_ref, gate_ref):
    x = h_ref[0]
    ms = jnp.mean(x * x, axis=-1, keepdims=True)
    hn = ((x * lax.rsqrt(ms + EPS)) * n1_ref[...]).astype(bf16)
    pa = _dot(hn, w_ref[:, 0:ATT_W + 2 * KV_W])
    lo = lax.broadcasted_iota(jnp.int32, (1, BLOCK), 1) < HEAD_DIM
    pieces = []
    for c in range(5):
        xc = pa[:, c * BLOCK:(c + 1) * BLOCK]
        msq = _seg64_mean(xc * xc, lo)
        pieces.append((xc * lax.rsqrt(msq + EPS)) * qkw_ref[c:c + 1, :])
    k = pieces[4]
    v = pa[:, ATT_W + KV_W:ATT_W + 2 * KV_W]
    k_sw = pltpu.roll(k, HEAD_DIM, 1)
    v_sw = pltpu.roll(v, HEAD_DIM, 1)
    out = pieces[:4] + [jnp.where(lo, k, k_sw), jnp.where(lo, k_sw, k),
                        jnp.where(lo, v, v_sw), jnp.where(lo, v_sw, v)]
    att_ref[0] = jnp.concatenate(out, axis=-1).astype(bf16)
    rest_ref[0] = _dot(hn, w_ref[:, ATT_W + 2 * KV_W:MAIN_W])
    gate_ref[0] = _dot(hn, wg_ref[...])


def _project(h, n1, w_main, w_gate, qkw, tm):
    B, Lp, _ = h.shape
    grid = (B, Lp // tm)
    return pl.pallas_call(
        _proj_kernel,
        grid=grid,
        in_specs=[
            pl.BlockSpec((1, tm, D_MODEL), lambda b, i: (b, i, 0)),
            pl.BlockSpec((1, D_MODEL), lambda b, i: (0, 0)),
            pl.BlockSpec((D_MODEL, MAIN_W), lambda b, i: (0, 0)),
            pl.BlockSpec((D_MODEL, BLOCK), lambda b, i: (0, 0)),
            pl.BlockSpec((8, BLOCK), lambda b, i: (0, 0)),
        ],
        out_specs=[
            pl.BlockSpec((1, tm, ATT_OUT_W), lambda b, i: (b, i, 0)),
            pl.BlockSpec((1, tm, REST_W), lambda b, i: (b, i, 0)),
            pl.BlockSpec((1, tm, BLOCK), lambda b, i: (b, i, 0)),
        ],
        out_shape=[
            jax.ShapeDtypeStruct((B, Lp, ATT_OUT_W), bf16),
            jax.ShapeDtypeStruct((B, Lp, REST_W), f32),
            jax.ShapeDtypeStruct((B, Lp, BLOCK), f32),
        ],
        compiler_params=pltpu.CompilerParams(
            dimension_semantics=("parallel", "parallel"), vmem_limit_bytes=VMEM_LIMIT),
        name="proj",
    )(h, n1, w_main, w_gate, qkw)


def _attn_kernel(q_ref, kvp_ref, kvc_ref, kvn_ref, kvm_ref, bias_ref, sink_ref, o_ref, *, nblk):
    i = pl.program_id(1)
    nb = pl.num_programs(1) * nblk
    lo = lax.broadcasted_iota(jnp.int32, (1, BLOCK), 1) < HEAD_DIM
    top = lax.broadcasted_iota(jnp.int32, (BLOCK, 1), 0) < HEAD_DIM
    ppad = jnp.zeros((BLOCK - N_META, 4 * BLOCK), bf16)
    kvs = ([kvp_ref[0]] + [kvc_ref[0, u * BLOCK:(u + 1) * BLOCK, :] for u in range(nblk)] + [kvn_ref[0]])
    meta = kvm_ref[0]

    def v_t(blk, kh):
        return blk[:, (2 + kh) * BLOCK:(3 + kh) * BLOCK].astype(f32).T.astype(bf16)

    vts = [[v_t(blk, kh) for blk in kvs] for kh in range(ATT_KV_HEADS)]
    vt_meta = [v_t(jnp.concatenate([meta, jnp.zeros((BLOCK - N_META, 4 * BLOCK), bf16)], axis=0), kh)
               for kh in range(ATT_KV_HEADS)]

    def scores(t, kh):
        j = i * nblk + t
        var = jnp.where(j == 0, 1, jnp.where(j == 1, 2, jnp.where(j == nb - 1, 3, 0)))
        q = q_ref[0, t * BLOCK:(t + 1) * BLOCK, :]
        ksl = slice(kh * BLOCK, (kh + 1) * BLOCK)
        kcat = jnp.concatenate([kvs[t][:, ksl], kvs[t + 1][:, ksl], kvs[t + 2][:, ksl],
                                meta[:, ksl]], axis=0)
        qparts = []
        for g in range(4):
            grp = q[:, (kh * 2 + g // 2) * BLOCK:(kh * 2 + g // 2 + 1) * BLOCK]
            keep = lo if g % 2 == 0 else jnp.logical_not(lo)
            qparts.append(jnp.where(keep, grp, jnp.zeros_like(grp)))
        qg = jnp.concatenate(qparts, axis=0)
        return _dot_nt(kcat, qg) + bias_ref[var * ATT_KV_HEADS + kh]

    units = [(t, kh) for t in range(nblk) for kh in range(ATT_KV_HEADS)]
    s_next = scores(*units[0])
    outs = []
    for n, (t, kh) in enumerate(units):
        s = s_next
        if n + 1 < len(units):
            s_next = scores(*units[n + 1])
        vt_cat = jnp.concatenate([vts[kh][t], vts[kh][t + 1], vts[kh][t + 2], vt_meta[kh]], axis=1)
        sk = sink_ref[kh]
        m = jnp.maximum(jnp.max(s, axis=0, keepdims=True), sk)
        p = jnp.exp2(s - m)
        den = jnp.sum(p, axis=0, keepdims=True) + jnp.exp2(sk - m)
        p_b = jnp.concatenate([p.astype(bf16), ppad], axis=0)
        o_t = _dot(vt_cat, p_b) * (1.0 / den)
        for c in range(2):
            pair = jnp.where(top, o_t[:, (2 * c) * BLOCK:(2 * c + 1) * BLOCK],
                             o_t[:, (2 * c + 1) * BLOCK:(2 * c + 2) * BLOCK])
            outs.append(pair.T)
        if kh == ATT_KV_HEADS - 1:
            o_ref[0, t * BLOCK:(t + 1) * BLOCK, :] = jnp.concatenate(outs, axis=-1).astype(bf16)
            outs = []


def _attention(att, bias, sink, tm):
    B, Lp, _ = att.shape
    nblk = tm // BLOCK
    nt = Lp // tm
    assert Lp // BLOCK >= 4
    meta_row = PAD // N_META
    return pl.pallas_call(
        functools.partial(_attn_kernel, nblk=nblk),
        grid=(B, nt),
        in_specs=[
            pl.BlockSpec((1, tm, ATT_W), lambda b, i: (b, i, 0)),
            pl.BlockSpec((1, BLOCK, ATT_W), lambda b, i: (b, jnp.maximum(i * nblk - 1, 0), 1)),
            pl.BlockSpec((1, tm, ATT_W), lambda b, i: (b, i, 1)),
            pl.BlockSpec((1, BLOCK, ATT_W), lambda b, i: (b, jnp.minimum((i + 1) * nblk, nt * nblk - 1), 1)),
            pl.BlockSpec((1, N_META, ATT_W), lambda b, i: (b, meta_row, 1)),
            pl.BlockSpec((4 * ATT_KV_HEADS, N_KEYS, 4 * BLOCK), lambda b, i: (0, 0, 0),
                         pipeline_mode=pl.Buffered(1)),
            pl.BlockSpec((ATT_KV_HEADS, 1, 4 * BLOCK), lambda b, i: (0, 0, 0)),
        ],
        out_specs=pl.BlockSpec((1, tm, ATT_W), lambda b, i: (b, i, 0)),
        out_shape=jax.ShapeDtypeStruct((B, Lp, ATT_W), bf16),
        compiler_params=pltpu.CompilerParams(
            dimension_semantics=("parallel", "parallel"), vmem_limit_bytes=VMEM_LIMIT),
        name="attn",
    )(att, att, att, att, att, bias, sink)


def _attention_bias():
    i = jnp.arange(BLOCK)[:, None]
    c = jnp.arange(4 * BLOCK)[None, :]
    dist = jnp.abs(i + BLOCK - c)
    band = (c < 3 * BLOCK) & (dist <= BLOCK)
    meta = (c >= 3 * BLOCK) & (c < 3 * BLOCK + N_META)
    slopes = jnp.exp2(-8.0 * jnp.arange(1, ATT_HEADS + 1, dtype=f32) / ATT_HEADS)
    blk = c // BLOCK
    variants = []
    for dead in ((), (0, 1), (0,), (2,)):
        ok = band
        for d in dead:
            ok = ok & (blk != d)
        b = jnp.where(ok[None], -(slopes[:, None, None] * dist[None].astype(f32)) * LOG2E,
                      jnp.where(meta[None], 0.0, NEG))
        variants.append(b.reshape(ATT_KV_HEADS, 4 * BLOCK, 4 * BLOCK).transpose(0, 2, 1)[:, :N_KEYS])
    return jnp.concatenate(variants, axis=0).astype(f32)


def _conv3_silu(x, prev8, nxt8, w3):
    tm = x.shape[0]
    rows = lax.broadcasted_iota(jnp.int32, (tm, 1), 0)
    up = jnp.where(rows == 0, prev8[7:8, :], pltpu.roll(x, 1, 0))
    dn = jnp.where(rows == tm - 1, nxt8[0:1, :], pltpu.roll(x, tm - 1, 0))
    y = (up * w3[0:1, :] + x * w3[1:2, :]) + dn * w3[2:3, :]
    return y * _sigmoid(y)


def _rows_to_heads(rows):
    return jnp.concatenate([jnp.broadcast_to(r, (HEAD_DIM, r.shape[1])) for r in rows], axis=0)


def _bd_lanes(xT_b):
    row_head = lax.broadcasted_iota(jnp.int32, (HW, 1), 0) >> 6
    return jnp.concatenate([jnp.where(row_head == h, xT_b, jnp.zeros_like(xT_b)) for h in range(4)], axis=1)


def _gate_rows(gates, gbias):
    gt = (gates + gbias).T[0:16, :]
    grow = lax.broadcasted_iota(jnp.int32, (16, 1), 0)
    PT = jnp.where((grow & 4) != 0, _log_sigmoid(gt), gt)
    r = lax.broadcasted_iota(jnp.int32, (BLOCK, 2 * BLOCK), 0)
    c = lax.broadcasted_iota(jnp.int32, (BLOCK, 2 * BLOCK), 1)
    tri = ((c < BLOCK) & (r <= c)) | ((c >= BLOCK) & (r >= c - BLOCK))
    cum2 = _dot01_right(PT, jnp.where(tri, 1.0, 0.0).astype(bf16))
    row = lax.broadcasted_iota(jnp.int32, (16, 1), 0)
    cum = jnp.where(row < 8, cum2[:, 0:BLOCK], cum2[:, BLOCK:2 * BLOCK])
    cum_al = pltpu.roll(cum, 12, 0)
    total = jnp.where(row < 8, cum_al[:, BLOCK - 1:BLOCK], cum_al[:, 0:1])
    return PT, cum_al, total


def _mlstm_state_step(r0, PT, cum_al, total, kT_b, vT, S_ref, N_ref, M_ref, bd):
    wend = (total - cum_al) + PT
    m_loc = jnp.max(wend, axis=-1, keepdims=True)
    e_end = jnp.exp(wend - m_loc)
    m_old = M_ref[:, 0:1]
    m_new = jnp.maximum(total + m_old, m_loc)
    a_rep = jnp.broadcast_to(jnp.exp((total + m_old) - m_new), (16, BLOCK))
    c_rep = jnp.broadcast_to(jnp.exp(m_loc - m_new), (16, BLOCK))
    M_ref[...] = jnp.broadcast_to(m_new, (16, BLOCK))
    e_heads = _rows_to_heads([e_end[r0 + h:r0 + h + 1, :] for h in range(4)])
    upd = _dot_nt((vT * e_heads).astype(bf16), kT_b)
    a_m = _rows_to_heads([a_rep[r0 + h:r0 + h + 1, :] for h in range(4)])
    c_m = _rows_to_heads([c_rep[r0 + h:r0 + h + 1, :] for h in range(4)])
    a_m2 = jnp.concatenate([a_m, a_m], axis=1)
    c_m2 = jnp.concatenate([c_m, c_m], axis=1)
    S_ref[...] = a_m2 * S_ref[...] + c_m2 * jnp.where(bd, upd, 0.0)
    n_loc = _dot_nt(e_end.astype(bf16), kT_b)
    lane_head = lax.broadcasted_iota(jnp.int32, (8, HW), 1) >> 6
    row8 = lax.broadcasted_iota(jnp.int32, (8, HW), 0)
    a8 = jnp.concatenate([a_rep[r0:r0 + 8, :], a_rep[r0:r0 + 8, :]], axis=1)
    c8 = jnp.concatenate([c_rep[r0:r0 + 8, :], c_rep[r0:r0 + 8, :]], axis=1)
    N_ref[0:8, :] = a8 * N_ref[0:8, :] + c8 * jnp.where(lane_head == row8, n_loc[r0:r0 + 8, :], 0.0)


def _mlstm_inputs(rest_ref, prev_ref, next_ref, cw_ref, first, last):
    x = rest_ref[0, :, 4 * HW:6 * HW]
    tm = x.shape[0]
    prev8 = jnp.where(first, 0.0, prev_ref[0])
    nxt8 = jnp.where(last, 0.0, next_ref[0])
    y = _conv3_silu(x, prev8, nxt8, cw_ref[...])
    rows = lax.broadcasted_iota(jnp.int32, (tm, 1), 0)
    keep = jnp.logical_or(jnp.logical_not(first), rows >= PAD)
    mq = y[:, 0:HW]
    mk = jnp.where(keep, y[:, HW:2 * HW], 0.0) * (HEAD_DIM ** -0.5)
    return mq, mk


def _bwd_state_kernel(rk_ref, rv_ref, mkx_ref, mv_ref, prev_ref, next_ref, gate_ref, cw_ref, gb_ref, lgb_ref,
                      r_out, s_out, n_out, m_out,
                      R_ref, S_ref, N_ref, M_ref, dk_ref, *, nchunk):
    i = pl.program_id(1)
    nt = pl.num_programs(1)
    ti = nt - 1 - i

    @pl.when(i == 0)
    def _():
        R_ref[...] = jnp.zeros_like(R_ref)
        S_ref[...] = jnp.zeros_like(S_ref)
        N_ref[...] = jnp.zeros_like(N_ref)
        M_ref[...] = jnp.zeros_like(M_ref)
        pos = lax.broadcasted_iota(jnp.int32, (1, BLOCK), 1).astype(f32)
        dk_ref[...] = jnp.exp(lgb_ref[...] * pos)

    bd = _bd_mask()
    g1 = jnp.exp(lgb_ref[...] * float(BLOCK))
    g_blk = jnp.concatenate([g1, g1], axis=1)
    first = ti == 0
    y = _conv3_silu(mkx_ref[0], jnp.where(first, 0.0, prev_ref[0]), jnp.where(ti == nt - 1, 0.0, next_ref[0]),
                    cw_ref[:, HW:2 * HW])
    trow = lax.broadcasted_iota(jnp.int32, (y.shape[0], 1), 0)
    mk_all = jnp.where(jnp.logical_or(jnp.logical_not(first), trow >= PAD), y, 0.0) * (HEAD_DIM ** -0.5)

    def prep(c):
        rows = slice(c * BLOCK, (c + 1) * BLOCK)
        kT_b = (rk_ref[0, rows, :] * (HEAD_DIM ** -0.5)).T.astype(bf16)
        vT = rv_ref[0, rows, :].T
        mkT_b = mk_all[rows].T.astype(bf16)
        mvT = mv_ref[0, rows, :].T
        return kT_b, vT, mkT_b, mvT, _gate_rows(gate_ref[0, rows, :], gb_ref[...])

    order = list(reversed(range(nchunk)))
    nxt = prep(order[0])
    for n, c in enumerate(order):
        kT_b, vT, mkT_b, mvT, (PT, cum_al, total) = nxt
        if n + 1 < nchunk:
            nxt = prep(order[n + 1])
        r_out[0, c] = R_ref[...].astype(bf16)
        s_out[0, c] = S_ref[...].astype(bf16)
        n_out[0, c] = N_ref[...].astype(bf16)
        m_out[0, c] = M_ref[...]
        R_ref[...] = R_ref[...] * g_blk + jnp.where(bd, _dot_nt((vT * dk_ref[...]).astype(bf16), kT_b), 0.0)
        _mlstm_state_step(8, PT, cum_al, total, mkT_b, mvT, S_ref, N_ref, M_ref, bd)


def _head_layernorm_t(x, w_t):
    outs = []
    for h in range(4):
        xh = x[h * HEAD_DIM:(h + 1) * HEAD_DIM, :]
        mu = jnp.mean(xh, axis=0, keepdims=True)
        d = xh - mu
        var = jnp.mean(d * d, axis=0, keepdims=True)
        outs.append(d * lax.rsqrt(var + EPS))
    return jnp.concatenate(outs, axis=0) * w_t


def _mlstm_direction(r0, mask, s_t, PT, cum_al, m_prev, vT_bd, S_b, N_b, q_b):
    qn = _dot_nt(N_b, q_b)
    ws, a_rows, r_rows = [], [], []
    for h in range(4):
        rr = r0 + h
        b_row = cum_al[rr:rr + 1, :]
        cb = jnp.broadcast_to(PT[rr:rr + 1, :] - b_row, (BLOCK, BLOCK)).T
        dl = jnp.where(mask, b_row + cb, NEG)
        m_intra = jnp.max(dl, axis=0, keepdims=True)
        inter = b_row + m_prev[rr:rr + 1, :]
        m_t = jnp.maximum(m_intra, inter)
        a_i = jnp.exp(inter - m_t)
        w = jnp.exp(dl - m_t) * s_t[h * BLOCK:(h + 1) * BLOCK, :]
        den = jnp.sum(w, axis=0, keepdims=True) + a_i * qn[h:h + 1, :]
        r_rows.append(1.0 / jnp.maximum(jnp.abs(den), jnp.exp(-m_t)))
        a_rows.append(a_i)
        ws.append(w.astype(bf16))
    num = _dot(vT_bd, jnp.concatenate(ws, axis=0)) + _rows_to_heads(a_rows) * _dot_nt(S_b, q_b)
    return num * _rows_to_heads(r_rows)


def _fwd_kernel(rest_ref, prev_ref, next_ref, gate_ref, rb_ref, sb_ref, nb_ref, mb_ref,
                cw_ref, gb_ref, lgf_ref, lgb_ref, lgf4_ref, lgb4_ref, rnw_ref, mnw_ref,
                h_ref, ao_ref, wout_ref,
                o_ref,
                R_ref, S_ref, N_ref, M_ref, dk_ref, dqf_ref, dqb_ref, dcomb_ref, mix_ref,
                *, nchunk):
    i = pl.program_id(1)
    nt = pl.num_programs(1) - 1
    slot = lax.rem(i, 2)

    @pl.when(i == 0)
    def _():
        mix_ref[...] = jnp.zeros_like(mix_ref)
        R_ref[...] = jnp.zeros_like(R_ref)
        S_ref[...] = jnp.zeros_like(S_ref)
        N_ref[...] = jnp.zeros_like(N_ref)
        M_ref[...] = jnp.zeros_like(M_ref)
        pos = lax.broadcasted_iota(jnp.int32, (1, BLOCK), 1).astype(f32)
        dk_ref[...] = jnp.exp(lgf_ref[...] * ((BLOCK - 1.0) - pos))
        dqf_ref[...] = jnp.exp(lgf_ref[...] * (pos + 1.0))
        dqb_ref[...] = jnp.exp(lgb_ref[...] * (float(BLOCK) - pos))
        key = lax.broadcasted_iota(jnp.int32, (4 * BLOCK, BLOCK), 0) & (BLOCK - 1)
        qry = lax.broadcasted_iota(jnp.int32, (4 * BLOCK, BLOCK), 1)
        d = (qry - key).astype(f32)
        dcomb_ref[...] = (jnp.where(d >= 0, jnp.exp(lgf4_ref[...] * jnp.maximum(d, 0.0)), 0.0)
                          + jnp.where(d <= 0, jnp.exp(lgb4_ref[...] * jnp.maximum(-d, 0.0)), 0.0))

    bd = _bd_mask()
    g1 = jnp.exp(lgf_ref[...] * float(BLOCK))
    g_blk = jnp.concatenate([g1, g1], axis=1)
    key_i = lax.broadcasted_iota(jnp.int32, (BLOCK, BLOCK), 0)
    qry_i = lax.broadcasted_iota(jnp.int32, (BLOCK, BLOCK), 1)
    mq_all, mk_all = _mlstm_inputs(rest_ref, prev_ref, next_ref, cw_ref, i == 0, i >= nt - 1)

    x_att = ao_ref[0]
    x_mix = mix_ref[1 - slot]
    trow = lax.broadcasted_iota(jnp.int32, (x_att.shape[0], 1), 0)
    keep_rows = jnp.logical_or(i > 1, trow >= PAD)

    def finish_prev(j):
        cols = slice(j * HW, (j + 1) * HW)
        y = h_ref[0, :, cols] + (_dot(x_att, wout_ref[0:ATT_W, cols]) + _dot(x_mix, wout_ref[ATT_W:, cols]))
        o_ref[0, :, cols] = jnp.where(keep_rows, y, 0.0)

    n_piece = D_MODEL // HW

    def prep(c):
        rows = slice(c * BLOCK, (c + 1) * BLOCK)
        q_b = rest_ref[0, rows, 0:HW].astype(bf16)
        k = rest_ref[0, rows, HW:2 * HW] * (HEAD_DIM ** -0.5)
        k_b = k.astype(bf16)
        kT_b = k.T.astype(bf16)
        vT = rest_ref[0, rows, 2 * HW:3 * HW].T
        a_t = (_dot_nt(_block_stack(k_b), q_b) * dcomb_ref[...]).astype(bf16)
        ro_t = _dot(_bd_lanes(vT.astype(bf16)), a_t)
        mq_b = mq_all[rows].astype(bf16)
        mk = mk_all[rows]
        mkT_b = mk.T.astype(bf16)
        mvT = rest_ref[0, rows, 6 * HW:7 * HW].T
        ms_t = _dot_nt(_block_stack(mk.astype(bf16)), mq_b)
        gr = _gate_rows(gate_ref[0, rows, :], gb_ref[...])
        return q_b, kT_b, vT, ro_t, mq_b, mkT_b, mvT, _bd_lanes(mvT.astype(bf16)), ms_t, gr

    nxt = prep(0)
    for c in range(nchunk):
        rows = slice(c * BLOCK, (c + 1) * BLOCK)
        q_b, kT_b, vT, ro_t, mq_b, mkT_b, mvT, mvT_bd, ms_t, (PT, cum_al, total) = nxt
        if c + 1 < nchunk:
            nxt = prep(c + 1)
        for j in range(c * n_piece // nchunk, (c + 1) * n_piece // nchunk):
            finish_prev(j)

        ro_t = ro_t + dqf_ref[...] * _dot_nt(R_ref[...].astype(bf16), q_b)
        ro_t = ro_t + dqb_ref[...] * _dot_nt(rb_ref[0, c], q_b)
        R_ref[...] = R_ref[...] * g_blk + jnp.where(bd, _dot_nt((vT * dk_ref[...]).astype(bf16), kT_b), 0.0)
        rg = rest_ref[0, rows, 3 * HW:4 * HW]
        ret = (rg * _sigmoid(rg)) * _head_layernorm_t(ro_t, rnw_ref[...]).T

        hf = _mlstm_direction(0, key_i <= qry_i, ms_t, PT, cum_al, M_ref[...], mvT_bd,
                              S_ref[...].astype(bf16), N_ref[...].astype(bf16), mq_b)
        hb = _mlstm_direction(8, key_i >= qry_i, ms_t, PT, cum_al, mb_ref[0, c], mvT_bd,
                              sb_ref[0, c], nb_ref[0, c], mq_b)
        mo = rest_ref[0, rows, 7 * HW:8 * HW]
        ml = _sigmoid(mo) * _head_layernorm_t(hf + hb, mnw_ref[...]).T
        _mlstm_state_step(0, PT, cum_al, total, mkT_b, mvT, S_ref, N_ref, M_ref, bd)

        mix_ref[slot, rows, :] = jnp.concatenate([ret, ml], axis=-1).astype(bf16)


def _bwd_states(rest, gates, cw, gb, lgb, tm):
    B, Lp, _ = rest.shape
    nt = Lp // tm
    nchunk = tm // BLOCK
    rev = lambda f: (lambda b, i: f(b, nt - 1 - i))
    r8 = tm // 8
    col = lambda j: pl.BlockSpec((1, tm, HW), lambda b, i: (b, nt - 1 - i, j))
    const2 = lambda b, i: (0, 0)
    state_spec = lambda r, c: pl.BlockSpec((1, nchunk, r, c), lambda b, i: (b, nt - 1 - i, 0, 0))
    nc = Lp // BLOCK
    return pl.pallas_call(
        functools.partial(_bwd_state_kernel, nchunk=nchunk),
        grid=(B, nt),
        in_specs=[
            col(1), col(2), col(5), col(6),
            pl.BlockSpec((1, 8, HW), rev(lambda b, t: (b, jnp.maximum(t * r8 - 1, 0), 5))),
            pl.BlockSpec((1, 8, HW), rev(lambda b, t: (b, jnp.minimum((t + 1) * r8, nt * r8 - 1), 5))),
            pl.BlockSpec((1, tm, BLOCK), rev(lambda b, t: (b, t, 0))),
            pl.BlockSpec((3, 2 * HW), const2),
            pl.BlockSpec((1, BLOCK), const2),
            pl.BlockSpec((HW, BLOCK), const2),
        ],
        out_specs=[state_spec(HW, HW), state_spec(HW, HW), state_spec(16, HW), state_spec(16, BLOCK)],
        out_shape=[
            jax.ShapeDtypeStruct((B, nc, HW, HW), bf16),
            jax.ShapeDtypeStruct((B, nc, HW, HW), bf16),
            jax.ShapeDtypeStruct((B, nc, 16, HW), bf16),
            jax.ShapeDtypeStruct((B, nc, 16, BLOCK), f32),
        ],
        scratch_shapes=[
            pltpu.VMEM((HW, HW), f32), pltpu.VMEM((HW, HW), f32), pltpu.VMEM((16, HW), f32),
            pltpu.VMEM((16, BLOCK), f32), pltpu.VMEM((HW, BLOCK), f32),
        ],
        compiler_params=pltpu.CompilerParams(
            dimension_semantics=("parallel", "arbitrary"), vmem_limit_bytes=VMEM_LIMIT),
        name="bwd_states",
    )(rest, rest, rest, rest, rest, rest, gates, cw, gb, lgb)


def _fwd_sweep(rest, gates, rb, sb, nbw, mb, cw, gb, lgf, lgb, lgf4, lgb4, rnw, mnw, h, att_o, w_out, tm):
    B, Lp, _ = rest.shape
    nt = Lp // tm
    nchunk = tm // BLOCK
    r8 = tm // 8
    cur = lambda t: jnp.minimum(t, nt - 1)
    done = lambda t: jnp.maximum(t - 1, 0)
    const2 = lambda b, t: (0, 0)
    state_spec = lambda r, c: pl.BlockSpec((1, nchunk, r, c), lambda b, t: (b, cur(t), 0, 0))
    return pl.pallas_call(
        functools.partial(_fwd_kernel, nchunk=nchunk),
        grid=(B, nt + 1),
        in_specs=[
            pl.BlockSpec((1, tm, REST_W), lambda b, t: (b, cur(t), 0)),
            pl.BlockSpec((1, 8, 2 * HW), lambda b, t: (b, jnp.maximum(cur(t) * r8 - 1, 0), 2)),
            pl.BlockSpec((1, 8, 2 * HW), lambda b, t: (b, jnp.minimum((cur(t) + 1) * r8, nt * r8 - 1), 2)),
            pl.BlockSpec((1, tm, BLOCK), lambda b, t: (b, cur(t), 0)),
            state_spec(HW, HW), state_spec(HW, HW), state_spec(16, HW), state_spec(16, BLOCK),
            pl.BlockSpec((3, 2 * HW), const2),
            pl.BlockSpec((1, BLOCK), const2),
            pl.BlockSpec((HW, BLOCK), const2),
            pl.BlockSpec((HW, BLOCK), const2),
            pl.BlockSpec((4 * BLOCK, BLOCK), const2),
            pl.BlockSpec((4 * BLOCK, BLOCK), const2),
            pl.BlockSpec((HW, BLOCK), const2),
            pl.BlockSpec((HW, BLOCK), const2),
            pl.BlockSpec((1, tm, D_MODEL), lambda b, t: (b, done(t), 0)),
            pl.BlockSpec((1, tm, ATT_W), lambda b, t: (b, done(t), 0)),
            pl.BlockSpec((D_MODEL, D_MODEL), const2),
        ],
        out_specs=pl.BlockSpec((1, tm, D_MODEL), lambda b, t: (b, done(t), 0)),
        out_shape=jax.ShapeDtypeStruct((B, Lp, D_MODEL), f32),
        scratch_shapes=[
            pltpu.VMEM((HW, HW), f32), pltpu.VMEM((HW, HW), f32), pltpu.VMEM((16, HW), f32),
            pltpu.VMEM((16, BLOCK), f32),
            pltpu.VMEM((HW, BLOCK), f32), pltpu.VMEM((HW, BLOCK), f32), pltpu.VMEM((HW, BLOCK), f32),
            pltpu.VMEM((4 * BLOCK, BLOCK), f32),
            pltpu.VMEM((2, tm, 2 * HW), bf16),
        ],
        compiler_params=pltpu.CompilerParams(
            dimension_semantics=("parallel", "arbitrary"), vmem_limit_bytes=VMEM_LIMIT),
        name="fwd_sweep",
    )(rest, rest, rest, gates, rb, sb, nbw, mb, cw, gb, lgf, lgb, lgf4, lgb4, rnw, mnw, h, att_o, w_out)


def _ffn_kernel(h_ref, hp_ref, hn_ref, n2_ref, wup_ref, cw_ref, wdn_ref, o_ref, act_ref):
    i = pl.program_id(1)
    last = i == pl.num_programs(1) - 1
    nw = n2_ref[...]

    def norm(x):
        ms = jnp.mean(x * x, axis=-1, keepdims=True)
        return ((x * lax.rsqrt(ms + EPS)) * nw).astype(bf16)

    x = h_ref[0]
    tm = x.shape[0]
    xe = norm(jnp.concatenate([x, hp_ref[0], jnp.where(last, 0.0, hn_ref[0])], axis=0))
    rows = lax.broadcasted_iota(jnp.int32, (tm, 1), 0)
    first_row = rows == 0
    last_row = rows == tm - 1

    def conv(ue, w3):
        u = ue[0:tm]
        up = jnp.where(first_row, ue[tm + 7:tm + 8, :], pltpu.roll(u, 1, 0))
        dn = jnp.where(last_row, ue[tm + 8:tm + 9, :], pltpu.roll(u, tm - 1, 0))
        return (up * w3[0:1, :] + u * w3[1:2, :]) + dn * w3[2:3, :]

    nck = D_FF // FF_CHUNK

    def up(c):
        gs = slice(c * FF_CHUNK, (c + 1) * FF_CHUNK)
        vs = slice(D_FF + c * FF_CHUNK, D_FF + (c + 1) * FF_CHUNK)
        return _dot(xe, wup_ref[:, gs]), _dot(xe, wup_ref[:, vs])

    u_next = up(0)
    acc = None
    done = 0
    for c in range(nck):
        ug, uv = u_next
        if c + 1 < nck:
            u_next = up(c + 1)
        gs = slice(c * FF_CHUNK, (c + 1) * FF_CHUNK)
        vs = slice(D_FF + c * FF_CHUNK, D_FF + (c + 1) * FF_CHUNK)
        g = conv(ug, cw_ref[:, gs])
        v = conv(uv, cw_ref[:, vs])
        act_ref[:, gs] = ((g * _sigmoid(g)) * v).astype(bf16)
        if (c + 1) % DOWN_GROUP == 0 or c + 1 == nck:
            ks = slice(done * FF_CHUNK, (c + 1) * FF_CHUNK)
            part = _dot(act_ref[:, ks], wdn_ref[ks, :])
            acc = part if acc is None else acc + part
            done = c + 1
    y = x + acc
    keep = jnp.logical_or(i > 0, rows >= PAD)
    o_ref[0] = jnp.where(keep, y, 0.0)


def _ffn(h, n2, w_up, cw, w_dn, tm):
    B, Lp, _ = h.shape
    nt = Lp // tm
    r8 = tm // 8
    return pl.pallas_call(
        _ffn_kernel,
        grid=(B, nt),
        in_specs=[
            pl.BlockSpec((1, tm, D_MODEL), lambda b, i: (b, i, 0)),
            pl.BlockSpec((1, 8, D_MODEL), lambda b, i: (b, jnp.maximum(i * r8 - 1, 0), 0)),
            pl.BlockSpec((1, 8, D_MODEL), lambda b, i: (b, jnp.minimum((i + 1) * r8, nt * r8 - 1), 0)),
            pl.BlockSpec((1, D_MODEL), lambda b, i: (0, 0)),
            pl.BlockSpec((D_MODEL, 2 * D_FF), lambda b, i: (0, 0), pipeline_mode=pl.Buffered(1)),
            pl.BlockSpec((3, 2 * D_FF), lambda b, i: (0, 0)),
            pl.BlockSpec((D_FF, D_MODEL), lambda b, i: (0, 0), pipeline_mode=pl.Buffered(1)),
        ],
        out_specs=pl.BlockSpec((1, tm, D_MODEL), lambda b, i: (b, i, 0)),
        out_shape=jax.ShapeDtypeStruct(h.shape, f32),
        scratch_shapes=[pltpu.VMEM((tm, D_FF), bf16)],
        compiler_params=pltpu.CompilerParams(
            dimension_semantics=("parallel", "parallel"), vmem_limit_bytes=VMEM_LIMIT),
        name="ffn",
    )(h, h, h, n2, w_up, cw, w_dn)


def _row_groups(x, height):
    col = jnp.repeat(x, height, axis=-1)
    return jnp.broadcast_to(col[:, :, None], col.shape + (BLOCK,))


def _lane_bcast(w):
    return jnp.broadcast_to(w.astype(f32)[:, :, None], w.shape + (BLOCK,))


def _prepare_params(norm1_w, w_in, attn_q_norm_w, attn_k_norm_w, attn_sink, ret_decay_logit,
                    ret_norm_w, mlstm_conv_w, mlstm_gate_b, mlstm_norm_w, w_out, norm2_w,
                    ffn_up, ffn_conv_w, ffn_down):
    scale = HEAD_DIM ** -0.5
    qrow = jnp.tile(attn_q_norm_w.astype(f32) * (scale * LOG2E), (1, 2))
    krow = jnp.tile(attn_k_norm_w.astype(f32), (1, 2))
    qkw = jnp.concatenate([jnp.stack([qrow] * 4 + [krow], axis=1),
                           jnp.zeros((DEPTH, 3, BLOCK), f32)], axis=1)
    sink = jnp.repeat(attn_sink.astype(f32).reshape(DEPTH, ATT_KV_HEADS, 4) * LOG2E, BLOCK, axis=-1)
    log_gamma = jax.nn.log_sigmoid(ret_decay_logit.astype(f32))
    gb = jnp.concatenate([mlstm_gate_b.astype(f32), jnp.zeros((DEPTH, BLOCK - N_GATES), f32)], axis=-1)
    return dict(
        n1=norm1_w.astype(f32)[:, None, :],
        w_main=w_in[:, :, :MAIN_W].astype(bf16),
        w_gate=jnp.concatenate([w_in[:, :, MAIN_W:], jnp.zeros((DEPTH, D_MODEL, BLOCK - N_GATES), w_in.dtype)],
                               axis=-1).astype(bf16),
        qkw=qkw,
        sink=sink[:, :, None, :],
        lgf=_row_groups(log_gamma[:, 0], HEAD_DIM),
        lgb=_row_groups(log_gamma[:, 1], HEAD_DIM),
        lgf4=_row_groups(log_gamma[:, 0], BLOCK),
        lgb4=_row_groups(log_gamma[:, 1], BLOCK),
        rnw=_lane_bcast(ret_norm_w),
        cw=mlstm_conv_w.astype(f32),
        gb=gb[:, None, :],
        mnw=_lane_bcast(mlstm_norm_w),
        w_out=w_out.astype(bf16),
        n2=norm2_w.astype(f32)[:, None, :],
        w_up=ffn_up.astype(bf16),
        fcw=ffn_conv_w.astype(f32),
        w_dn=ffn_down.astype(bf16),
    )


def _trunk(x, meta_tokens, params, bias):
    B, seq, _ = x.shape
    lp = seq + BLOCK
    tm = _row_tile(lp)
    meta = jnp.broadcast_to(meta_tokens[None].astype(x.dtype), (B, N_META, D_MODEL))
    h = jnp.concatenate([jnp.zeros((B, PAD, D_MODEL), x.dtype), meta, x], axis=1)

    def layer(h, p):
        att, rest, gates = _project(h, p["n1"], p["w_main"], p["w_gate"], p["qkw"], tm)
        att_o = _attention(att, bias, p["sink"], tm)
        rb, sb, nbw, mb = _bwd_states(rest, gates, p["cw"], p["gb"], p["lgb"], tm)
        h = _fwd_sweep(rest, gates, rb, sb, nbw, mb, p["cw"], p["gb"], p["lgf"], p["lgb"],
                       p["lgf4"], p["lgb4"], p["rnw"], p["mnw"], h, att_o, p["w_out"], tm)
        h = _ffn(h, p["n2"], p["w_up"], p["fcw"], p["w_dn"], tm)
        return h, None

    h, _ = lax.scan(layer, h, params)
    return h[:, BLOCK:]


def kernel(x_prompt, x_sample, meta_tokens, norm1_w, w_in, attn_q_norm_w, attn_k_norm_w, attn_sink,
           ret_decay_logit, ret_norm_w, mlstm_conv_w, mlstm_gate_b, mlstm_norm_w, w_out, norm2_w,
           ffn_up, ffn_conv_w, ffn_down):
    params = _prepare_params(norm1_w, w_in, attn_q_norm_w, attn_k_norm_w, attn_sink, ret_decay_logit,
                             ret_norm_w, mlstm_conv_w, mlstm_gate_b, mlstm_norm_w, w_out, norm2_w,
                             ffn_up, ffn_conv_w, ffn_down)
    bias = _attention_bias()
    y_prompt = _trunk(x_prompt, meta_tokens, params, bias)
    y_sample = _trunk(x_sample, meta_tokens, params, bias)
    return (y_prompt, y_sample)
```

```python
import functools
import math

import jax
import jax.numpy as jnp
from jax import lax
from jax.experimental import pallas as pl
from jax.experimental.pallas import tpu as pltpu

f32 = jnp.float32
bf16 = jnp.bfloat16

D_MODEL = 1024
DEPTH = 4
N_META = 16
BLOCK = 128
PAD = BLOCK - N_META
HEAD_DIM = 64
ATT_HEADS = 8
ATT_KV_HEADS = 2
ATT_W = 512
KV_W = 128
HW = 256
N_GATES = 16
MAIN_W = 2816
REST_W = 2048
N_KEYS = 3 * BLOCK + N_META
ATT_OUT_W = 1024
D_FF = 2816
FF_CHUNK = 256
DOWN_GROUP = 11
EPS = 1e-6
LOG2E = 1.4426950408889634
NEG = -1e30
VMEM_LIMIT = 56 * 1024 * 1024


def _row_tile(lp):
    nb = lp // BLOCK
    best = 1
    for t in range(1, 7):
        if nb % t == 0:
            best = t
    return best * BLOCK


def _sigmoid(x):
    return 1.0 / (1.0 + jnp.exp(-x))


def _log_sigmoid(x):
    return jnp.minimum(x, 0.0) - jnp.log(1.0 + jnp.exp(-jnp.abs(x)))


def _seg64_mean(x, lo):
    a = jnp.sum(jnp.where(lo, x, 0.0), axis=-1, keepdims=True)
    b = jnp.sum(jnp.where(lo, 0.0, x), axis=-1, keepdims=True)
    return jnp.where(lo, a, b) * (1.0 / HEAD_DIM)


def _block_stack(x_b):
    lane = lax.broadcasted_iota(jnp.int32, (1, HW), 1)
    parts = []
    for h in range(4):
        keep = (lane >= h * HEAD_DIM) & (lane < (h + 1) * HEAD_DIM)
        parts.append(jnp.where(keep, x_b, jnp.zeros_like(x_b)))
    return jnp.concatenate(parts, axis=0)


def _bd_mask():
    r = lax.broadcasted_iota(jnp.int32, (HW, HW), 0) >> 6
    c = lax.broadcasted_iota(jnp.int32, (HW, HW), 1) >> 6
    return r == c


def _dot(a, b):
    return jnp.dot(a, b, preferred_element_type=f32)


def _dot_nt(a, b):
    return lax.dot_general(a, b, (((1,), (1,)), ((), ())), preferred_element_type=f32)


def _split3(x):
    p1 = x.astype(bf16)
    r1 = x - p1.astype(f32)
    p2 = r1.astype(bf16)
    p3 = (r1 - p2.astype(f32)).astype(bf16)
    return p1, p2, p3


def _dot01_right(x, m01):
    p1, p2, p3 = _split3(x)
    return (_dot(p3, m01) + _dot(p2, m01)) + _dot(p1, m01)


def _proj_kernel(h_ref, n1_ref, w_ref, wg_ref, qkw_ref, att_ref, rest_ref, gate_ref):
    x = h_ref[0]
    ms = jnp.mean(x * x, axis=-1, keepdims=True)
    hn = ((x * lax.rsqrt(ms + EPS)) * n1_ref[...]).astype(bf16)
    pa = _dot(hn, w_ref[:, 0:ATT_W + 2 * KV_W])
    lo = lax.broadcasted_iota(jnp.int32, (1, BLOCK), 1) < HEAD_DIM
    pieces = []
    for c in range(5):
        xc = pa[:, c * BLOCK:(c + 1) * BLOCK]
        msq = _seg64_mean(xc * xc, lo)
        pieces.append((xc * lax.rsqrt(msq + EPS)) * qkw_ref[c:c + 1, :])
    k = pieces[4]
    v = pa[:, ATT_W + KV_W:ATT_W + 2 * KV_W]
    k_sw = pltpu.roll(k, HEAD_DIM, 1)
    v_sw = pltpu.roll(v, HEAD_DIM, 1)
    out = pieces[:4] + [jnp.where(lo, k, k_sw), jnp.where(lo, k_sw, k),
                        jnp.where(lo, v, v_sw), jnp.where(lo, v_sw, v)]
    att_ref[0] = jnp.concatenate(out, axis=-1).astype(bf16)
    rest_ref[0] = _dot(hn, w_ref[:, ATT_W + 2 * KV_W:MAIN_W])
    gate_ref[0] = _dot(hn, wg_ref[...])


def _project(h, n1, w_main, w_gate, qkw, tm):
    B, Lp, _ = h.shape
    grid = (B, Lp // tm)
    return pl.pallas_call(
        _proj_kernel,
        grid=grid,
        in_specs=[
            pl.BlockSpec((1, tm, D_MODEL), lambda b, i: (b, i, 0)),
            pl.BlockSpec((1, D_MODEL), lambda b, i: (0, 0)),
            pl.BlockSpec((D_MODEL, MAIN_W), lambda b, i: (0, 0)),
            pl.BlockSpec((D_MODEL, BLOCK), lambda b, i: (0, 0)),
            pl.BlockSpec((8, BLOCK), lambda b, i: (0, 0)),
        ],
        out_specs=[
            pl.BlockSpec((1, tm, ATT_OUT_W), lambda b, i: (b, i, 0)),
            pl.BlockSpec((1, tm, REST_W), lambda b, i: (b, i, 0)),
            pl.BlockSpec((1, tm, BLOCK), lambda b, i: (b, i, 0)),
        ],
        out_shape=[
            jax.ShapeDtypeStruct((B, Lp, ATT_OUT_W), bf16),
            jax.ShapeDtypeStruct((B, Lp, REST_W), f32),
            jax.ShapeDtypeStruct((B, Lp, BLOCK), f32),
        ],
        compiler_params=pltpu.CompilerParams(
            dimension_semantics=("parallel", "parallel"), vmem_limit_bytes=VMEM_LIMIT),
        name="proj",
    )(h, n1, w_main, w_gate, qkw)


def _attn_kernel(q_ref, kvp_ref, kvc_ref, kvn_ref, kvm_ref, bias_ref, sink_ref, o_ref, *, nblk):
    i = pl.program_id(1)
    nb = pl.num_programs(1) * nblk
    lo = lax.broadcasted_iota(jnp.int32, (1, BLOCK), 1) < HEAD_DIM
    top = lax.broadcasted_iota(jnp.int32, (BLOCK, 1), 0) < HEAD_DIM
    ppad = jnp.zeros((BLOCK - N_META, 4 * BLOCK), bf16)
    kvs = ([kvp_ref[0]] + [kvc_ref[0, u * BLOCK:(u + 1) * BLOCK, :] for u in range(nblk)] + [kvn_ref[0]])
    meta = kvm_ref[0]

    def v_t(blk, kh):
        return blk[:, (2 + kh) * BLOCK:(3 + kh) * BLOCK].astype(f32).T.astype(bf16)

    vts = [[v_t(blk, kh) for blk in kvs] for kh in range(ATT_KV_HEADS)]
    vt_meta = [v_t(jnp.concatenate([meta, jnp.zeros((BLOCK - N_META, 4 * BLOCK), bf16)], axis=0), kh)
               for kh in range(ATT_KV_HEADS)]

    def scores(t, kh):
        j = i * nblk + t
        var = jnp.where(j == 0, 1, jnp.where(j == 1, 2, jnp.where(j == nb - 1, 3, 0)))
        q = q_ref[0, t * BLOCK:(t + 1) * BLOCK, :]
        ksl = slice(kh * BLOCK, (kh + 1) * BLOCK)
        kcat = jnp.concatenate([kvs[t][:, ksl], kvs[t + 1][:, ksl], kvs[t + 2][:, ksl],
                                meta[:, ksl]], axis=0)
        qparts = []
        for g in range(4):
            grp = q[:, (kh * 2 + g // 2) * BLOCK:(kh * 2 + g // 2 + 1) * BLOCK]
            keep = lo if g % 2 == 0 else jnp.logical_not(lo)
            qparts.append(jnp.where(keep, grp, jnp.zeros_like(grp)))
        qg = jnp.concatenate(qparts, axis=0)
        return _dot_nt(kcat, qg) + bias_ref[var * ATT_KV_HEADS + kh]

    units = [(t, kh) for t in range(nblk) for kh in range(ATT_KV_HEADS)]
    s_next = scores(*units[0])
    outs = []
    for n, (t, kh) in enumerate(units):
        s = s_next
        if n + 1 < len(units):
            s_next = scores(*units[n + 1])
        vt_cat = jnp.concatenate([vts[kh][t], vts[kh][t + 1], vts[kh][t + 2], vt_meta[kh]], axis=1)
        sk = sink_ref[kh]
        m = jnp.maximum(jnp.max(s, axis=0, keepdims=True), sk)
        p = jnp.exp2(s - m)
        den = jnp.sum(p, axis=0, keepdims=True) + jnp.exp2(sk - m)
        p_b = jnp.concatenate([p.astype(bf16), ppad], axis=0)
        o_t = _dot(vt_cat, p_b) * (1.0 / den)
        for c in range(2):
            pair = jnp.where(top, o_t[:, (2 * c) * BLOCK:(2 * c + 1) * BLOCK],
                             o_t[:, (2 * c + 1) * BLOCK:(2 * c + 2) * BLOCK])
            outs.append(pair.T)
        if kh == ATT_KV_HEADS - 1:
            o_ref[0, t * BLOCK:(t + 1) * BLOCK, :] = jnp.concatenate(outs, axis=-1).astype(bf16)
            outs = []


def _attention(att, bias, sink, tm):
    B, Lp, _ = att.shape
    nblk = tm // BLOCK
    nt = Lp // tm
    assert Lp // BLOCK >= 4
    meta_row = PAD // N_META
    return pl.pallas_call(
        functools.partial(_attn_kernel, nblk=nblk),
        grid=(B, nt),
        in_specs=[
            pl.BlockSpec((1, tm, ATT_W), lambda b, i: (b, i, 0)),
            pl.BlockSpec((1, BLOCK, ATT_W), lambda b, i: (b, jnp.maximum(i * nblk - 1, 0), 1)),
            pl.BlockSpec((1, tm, ATT_W), lambda b, i: (b, i, 1)),
            pl.BlockSpec((1, BLOCK, ATT_W), lambda b, i: (b, jnp.minimum((i + 1) * nblk, nt * nblk - 1), 1)),
            pl.BlockSpec((1, N_META, ATT_W), lambda b, i: (b, meta_row, 1)),
            pl.BlockSpec((4 * ATT_KV_HEADS, N_KEYS, 4 * BLOCK), lambda b, i: (0, 0, 0),
                         pipeline_mode=pl.Buffered(1)),
            pl.BlockSpec((ATT_KV_HEADS, 1, 4 * BLOCK), lambda b, i: (0, 0, 0)),
        ],
        out_specs=pl.BlockSpec((1, tm, ATT_W), lambda b, i: (b, i, 0)),
        out_shape=jax.ShapeDtypeStruct((B, Lp, ATT_W), bf16),
        compiler_params=pltpu.CompilerParams(
            dimension_semantics=("parallel", "parallel"), vmem_limit_bytes=VMEM_LIMIT),
        name="attn",
    )(att, att, att, att, att, bias, sink)


def _attention_bias():
    i = jnp.arange(BLOCK)[:, None]
    c = jnp.arange(4 * BLOCK)[None, :]
    dist = jnp.abs(i + BLOCK - c)
    band = (c < 3 * BLOCK) & (dist <= BLOCK)
    meta = (c >= 3 * BLOCK) & (c < 3 * BLOCK + N_META)
    slopes = jnp.exp2(-8.0 * jnp.arange(1, ATT_HEADS + 1, dtype=f32) / ATT_HEADS)
    blk = c // BLOCK
    variants = []
    for dead in ((), (0, 1), (0,), (2,)):
        ok = band
        for d in dead:
            ok = ok & (blk != d)
        b = jnp.where(ok[None], -(slopes[:, None, None] * dist[None].astype(f32)) * LOG2E,
                      jnp.where(meta[None], 0.0, NEG))
        variants.append(b.reshape(ATT_KV_HEADS, 4 * BLOCK, 4 * BLOCK).transpose(0, 2, 1)[:, :N_KEYS])
    return jnp.concatenate(variants, axis=0).astype(f32)


def _conv3_silu(x, prev8, nxt8, w3):
    tm = x.shape[0]
    rows = lax.broadcasted_iota(jnp.int32, (tm, 1), 0)
    up = jnp.where(rows == 0, prev8[7:8, :], pltpu.roll(x, 1, 0))
    dn = jnp.where(rows == tm - 1, nxt8[0:1, :], pltpu.roll(x, tm - 1, 0))
    y = (up * w3[0:1, :] + x * w3[1:2, :]) + dn * w3[2:3, :]
    return y * _sigmoid(y)


def _rows_to_heads(rows):
    return jnp.concatenate([jnp.broadcast_to(r, (HEAD_DIM, r.shape[1])) for r in rows], axis=0)


def _bd_lanes(xT_b):
    row_head = lax.broadcasted_iota(jnp.int32, (HW, 1), 0) >> 6
    return jnp.concatenate([jnp.where(row_head == h, xT_b, jnp.zeros_like(xT_b)) for h in range(4)], axis=1)


def _gate_rows(gates, gbias):
    gt = (gates + gbias).T[0:16, :]
    grow = lax.broadcasted_iota(jnp.int32, (16, 1), 0)
    PT = jnp.where((grow & 4) != 0, _log_sigmoid(gt), gt)
    r = lax.broadcasted_iota(jnp.int32, (BLOCK, 2 * BLOCK), 0)
    c = lax.broadcasted_iota(jnp.int32, (BLOCK, 2 * BLOCK), 1)
    tri = ((c < BLOCK) & (r <= c)) | ((c >= BLOCK) & (r >= c - BLOCK))
    cum2 = _dot01_right(PT, jnp.where(tri, 1.0, 0.0).astype(bf16))
    row = lax.broadcasted_iota(jnp.int32, (16, 1), 0)
    cum = jnp.where(row < 8, cum2[:, 0:BLOCK], cum2[:, BLOCK:2 * BLOCK])
    cum_al = pltpu.roll(cum, 12, 0)
    total = jnp.where(row < 8, cum_al[:, BLOCK - 1:BLOCK], cum_al[:, 0:1])
    return PT, cum_al, total


def _mlstm_state_step(r0, PT, cum_al, total, kT_b, vT, S_ref, N_ref, M_ref, bd):
    wend = (total - cum_al) + PT
    m_loc = jnp.max(wend, axis=-1, keepdims=True)
    e_end = jnp.exp(wend - m_loc)
    m_old = M_ref[:, 0:1]
    m_new = jnp.maximum(total + m_old, m_loc)
    a_rep = jnp.broadcast_to(jnp.exp((total + m_old) - m_new), (16, BLOCK))
    c_rep = jnp.broadcast_to(jnp.exp(m_loc - m_new), (16, BLOCK))
    M_ref[...] = jnp.broadcast_to(m_new, (16, BLOCK))
    e_heads = _rows_to_heads([e_end[r0 + h:r0 + h + 1, :] for h in range(4)])
    upd = _dot_nt((vT * e_heads).astype(bf16), kT_b)
    a_m = _rows_to_heads([a_rep[r0 + h:r0 + h + 1, :] for h in range(4)])
    c_m = _rows_to_heads([c_rep[r0 + h:r0 + h + 1, :] for h in range(4)])
    a_m2 = jnp.concatenate([a_m, a_m], axis=1)
    c_m2 = jnp.concatenate([c_m, c_m], axis=1)
    S_ref[...] = a_m2 * S_ref[...] + c_m2 * jnp.where(bd, upd, 0.0)
    n_loc = _dot_nt(e_end.astype(bf16), kT_b)
    lane_head = lax.broadcasted_iota(jnp.int32, (8, HW), 1) >> 6
    row8 = lax.broadcasted_iota(jnp.int32, (8, HW), 0)
    a8 = jnp.concatenate([a_rep[r0:r0 + 8, :], a_rep[r0:r0 + 8, :]], axis=1)
    c8 = jnp.concatenate([c_rep[r0:r0 + 8, :], c_rep[r0:r0 + 8, :]], axis=1)
    N_ref[0:8, :] = a8 * N_ref[0:8, :] + c8 * jnp.where(lane_head == row8, n_loc[r0:r0 + 8, :], 0.0)


def _mlstm_inputs(rest_ref, prev_ref, next_ref, cw_ref, first, last):
    x = rest_ref[0, :, 4 * HW:6 * HW]
    tm = x.shape[0]
    prev8 = jnp.where(first, 0.0, prev_ref[0])
    nxt8 = jnp.where(last, 0.0, next_ref[0])
    y = _conv3_silu(x, prev8, nxt8, cw_ref[...])
    rows = lax.broadcasted_iota(jnp.int32, (tm, 1), 0)
    keep = jnp.logical_or(jnp.logical_not(first), rows >= PAD)
    mq = y[:, 0:HW]
    mk = jnp.where(keep, y[:, HW:2 * HW], 0.0) * (HEAD_DIM ** -0.5)
    return mq, mk


def _bwd_state_kernel(rk_ref, rv_ref, mkx_ref, mv_ref, prev_ref, next_ref, gate_ref, cw_ref, gb_ref, lgb_ref,
                      r_out, s_out, n_out, m_out,
                      R_ref, S_ref, N_ref, M_ref, dk_ref, *, nchunk, nseq):
    i = pl.program_id(1)
    nt = pl.num_programs(1)
    ti = nt - 1 - i

    @pl.when(i == 0)
    def _():
        R_ref[...] = jnp.zeros_like(R_ref)
        S_ref[...] = jnp.zeros_like(S_ref)
        N_ref[...] = jnp.zeros_like(N_ref)
        M_ref[...] = jnp.zeros_like(M_ref)
        pos = lax.broadcasted_iota(jnp.int32, (1, BLOCK), 1).astype(f32)
        dk_ref[...] = jnp.exp(lgb_ref[...] * pos)

    bd = _bd_mask()
    g1 = jnp.exp(lgb_ref[...] * float(BLOCK))
    g_blk = jnp.concatenate([g1, g1], axis=1)
    first = ti == 0
    mk_all = []
    for s in range(nseq):
        y = _conv3_silu(mkx_ref[s], jnp.where(first, 0.0, prev_ref[s]),
                        jnp.where(ti == nt - 1, 0.0, next_ref[s]), cw_ref[:, HW:2 * HW])
        trow = lax.broadcasted_iota(jnp.int32, (y.shape[0], 1), 0)
        mk_all.append(jnp.where(jnp.logical_or(jnp.logical_not(first), trow >= PAD), y, 0.0) * (HEAD_DIM ** -0.5))

    def prep(c, s):
        rows = slice(c * BLOCK, (c + 1) * BLOCK)
        kT_b = (rk_ref[s, rows, :] * (HEAD_DIM ** -0.5)).T.astype(bf16)
        vT = rv_ref[s, rows, :].T
        mkT_b = mk_all[s][rows].T.astype(bf16)
        mvT = mv_ref[s, rows, :].T
        return kT_b, vT, mkT_b, mvT, _gate_rows(gate_ref[s, rows, :], gb_ref[...])

    order = [(c, s) for c in reversed(range(nchunk)) for s in range(nseq)]
    nxt = prep(*order[0])
    for n, (c, s) in enumerate(order):
        kT_b, vT, mkT_b, mvT, (PT, cum_al, total) = nxt
        if n + 1 < len(order):
            nxt = prep(*order[n + 1])
        r_out[s, c] = R_ref[s].astype(bf16)
        s_out[s, c] = S_ref[s].astype(bf16)
        n_out[s, c] = N_ref[s].astype(bf16)
        m_out[s, c] = M_ref[s]
        R_ref[s] = R_ref[s] * g_blk + jnp.where(bd, _dot_nt((vT * dk_ref[...]).astype(bf16), kT_b), 0.0)
        _mlstm_state_step(8, PT, cum_al, total, mkT_b, mvT, S_ref.at[s], N_ref.at[s], M_ref.at[s], bd)


def _head_layernorm_t(x, w_t):
    outs = []
    for h in range(4):
        xh = x[h * HEAD_DIM:(h + 1) * HEAD_DIM, :]
        mu = jnp.mean(xh, axis=0, keepdims=True)
        d = xh - mu
        var = jnp.mean(d * d, axis=0, keepdims=True)
        outs.append(d * lax.rsqrt(var + EPS))
    return jnp.concatenate(outs, axis=0) * w_t


def _mlstm_direction(r0, mask, s_t, PT, cum_al, m_prev, vT_bd, S_b, N_b, q_b):
    qn = _dot_nt(N_b, q_b)
    ws, a_rows, r_rows = [], [], []
    for h in range(4):
        rr = r0 + h
        b_row = cum_al[rr:rr + 1, :]
        cb = jnp.broadcast_to(PT[rr:rr + 1, :] - b_row, (BLOCK, BLOCK)).T
        dl = jnp.where(mask, b_row + cb, NEG)
        m_intra = jnp.max(dl, axis=0, keepdims=True)
        inter = b_row + m_prev[rr:rr + 1, :]
        m_t = jnp.maximum(m_intra, inter)
        a_i = jnp.exp(inter - m_t)
        w = jnp.exp(dl - m_t) * s_t[h * BLOCK:(h + 1) * BLOCK, :]
        den = jnp.sum(w, axis=0, keepdims=True) + a_i * qn[h:h + 1, :]
        r_rows.append(1.0 / jnp.maximum(jnp.abs(den), jnp.exp(-m_t)))
        a_rows.append(a_i)
        ws.append(w.astype(bf16))
    num = _dot(vT_bd, jnp.concatenate(ws, axis=0)) + _rows_to_heads(a_rows) * _dot_nt(S_b, q_b)
    return num * _rows_to_heads(r_rows)


def _fwd_kernel(rest_ref, prev_ref, next_ref, gate_ref, rb_ref, sb_ref, nb_ref, mb_ref,
                cw_ref, gb_ref, lgf_ref, lgb_ref, lgf4_ref, lgb4_ref, rnw_ref, mnw_ref,
                h_ref, ao_ref, wout_ref,
                o_ref,
                R_ref, S_ref, N_ref, M_ref, dk_ref, dqf_ref, dqb_ref, dcomb_ref, mix_ref,
                *, nchunk, nt):
    g = pl.program_id(0)
    n_tiles = pl.num_programs(0) - 1
    i = lax.rem(jnp.minimum(g, n_tiles - 1), nt)
    done_first = lax.rem(jnp.maximum(g - 1, 0), nt) == 0
    slot = lax.rem(g, 2)

    @pl.when(i == 0)
    def _():
        R_ref[...] = jnp.zeros_like(R_ref)
        S_ref[...] = jnp.zeros_like(S_ref)
        N_ref[...] = jnp.zeros_like(N_ref)
        M_ref[...] = jnp.zeros_like(M_ref)

    @pl.when(g == 0)
    def _():
        mix_ref[...] = jnp.zeros_like(mix_ref)
        pos = lax.broadcasted_iota(jnp.int32, (1, BLOCK), 1).astype(f32)
        dk_ref[...] = jnp.exp(lgf_ref[...] * ((BLOCK - 1.0) - pos))
        dqf_ref[...] = jnp.exp(lgf_ref[...] * (pos + 1.0))
        dqb_ref[...] = jnp.exp(lgb_ref[...] * (float(BLOCK) - pos))
        key = lax.broadcasted_iota(jnp.int32, (4 * BLOCK, BLOCK), 0) & (BLOCK - 1)
        qry = lax.broadcasted_iota(jnp.int32, (4 * BLOCK, BLOCK), 1)
        d = (qry - key).astype(f32)
        dcomb_ref[...] = (jnp.where(d >= 0, jnp.exp(lgf4_ref[...] * jnp.maximum(d, 0.0)), 0.0)
                          + jnp.where(d <= 0, jnp.exp(lgb4_ref[...] * jnp.maximum(-d, 0.0)), 0.0))

    bd = _bd_mask()
    g1 = jnp.exp(lgf_ref[...] * float(BLOCK))
    g_blk = jnp.concatenate([g1, g1], axis=1)
    key_i = lax.broadcasted_iota(jnp.int32, (BLOCK, BLOCK), 0)
    qry_i = lax.broadcasted_iota(jnp.int32, (BLOCK, BLOCK), 1)
    mq_all, mk_all = _mlstm_inputs(rest_ref, prev_ref, next_ref, cw_ref, i == 0, i >= nt - 1)

    x_att = ao_ref[0]
    x_mix = mix_ref[1 - slot]
    trow = lax.broadcasted_iota(jnp.int32, (x_att.shape[0], 1), 0)
    keep_rows = jnp.logical_or(jnp.logical_not(done_first), trow >= PAD)

    def finish_prev(j):
        cols = slice(j * HW, (j + 1) * HW)
        y = h_ref[0, :, cols] + (_dot(x_att, wout_ref[0:ATT_W, cols]) + _dot(x_mix, wout_ref[ATT_W:, cols]))
        o_ref[0, :, cols] = jnp.where(keep_rows, y, 0.0)

    n_piece = D_MODEL // HW

    def prep(c):
        rows = slice(c * BLOCK, (c + 1) * BLOCK)
        q_b = rest_ref[0, rows, 0:HW].astype(bf16)
        k = rest_ref[0, rows, HW:2 * HW] * (HEAD_DIM ** -0.5)
        k_b = k.astype(bf16)
        kT_b = k.T.astype(bf16)
        vT = rest_ref[0, rows, 2 * HW:3 * HW].T
        a_t = (_dot_nt(_block_stack(k_b), q_b) * dcomb_ref[...]).astype(bf16)
        ro_t = _dot(_bd_lanes(vT.astype(bf16)), a_t)
        mq_b = mq_all[rows].astype(bf16)
        mk = mk_all[rows]
        mkT_b = mk.T.astype(bf16)
        mvT = rest_ref[0, rows, 6 * HW:7 * HW].T
        ms_t = _dot_nt(_block_stack(mk.astype(bf16)), mq_b)
        gr = _gate_rows(gate_ref[0, rows, :], gb_ref[...])
        return q_b, kT_b, vT, ro_t, mq_b, mkT_b, mvT, _bd_lanes(mvT.astype(bf16)), ms_t, gr

    nxt = prep(0)
    for c in range(nchunk):
        rows = slice(c * BLOCK, (c + 1) * BLOCK)
        q_b, kT_b, vT, ro_t, mq_b, mkT_b, mvT, mvT_bd, ms_t, (PT, cum_al, total) = nxt
        if c + 1 < nchunk:
            nxt = prep(c + 1)
        for j in range(c * n_piece // nchunk, (c + 1) * n_piece // nchunk):
            finish_prev(j)

        ro_t = ro_t + dqf_ref[...] * _dot_nt(R_ref[...].astype(bf16), q_b)
        ro_t = ro_t + dqb_ref[...] * _dot_nt(rb_ref[0, c], q_b)
        R_ref[...] = R_ref[...] * g_blk + jnp.where(bd, _dot_nt((vT * dk_ref[...]).astype(bf16), kT_b), 0.0)
        rg = rest_ref[0, rows, 3 * HW:4 * HW]
        ret = (rg * _sigmoid(rg)) * _head_layernorm_t(ro_t, rnw_ref[...]).T

        hf = _mlstm_direction(0, key_i <= qry_i, ms_t, PT, cum_al, M_ref[...], mvT_bd,
                              S_ref[...].astype(bf16), N_ref[...].astype(bf16), mq_b)
        hb = _mlstm_direction(8, key_i >= qry_i, ms_t, PT, cum_al, mb_ref[0, c], mvT_bd,
                              sb_ref[0, c], nb_ref[0, c], mq_b)
        mo = rest_ref[0, rows, 7 * HW:8 * HW]
        ml = _sigmoid(mo) * _head_layernorm_t(hf + hb, mnw_ref[...]).T
        _mlstm_state_step(0, PT, cum_al, total, mkT_b, mvT, S_ref, N_ref, M_ref, bd)

        mix_ref[slot, rows, :] = jnp.concatenate([ret, ml], axis=-1).astype(bf16)


def _bwd_states(rest, gates, cw, gb, lgb, tm):
    B, Lp, _ = rest.shape
    nt = Lp // tm
    nchunk = tm // BLOCK
    nseq = 2 if B % 2 == 0 else 1
    rev = lambda f: (lambda b, i: f(b, nt - 1 - i))
    r8 = tm // 8
    col = lambda j: pl.BlockSpec((nseq, tm, HW), lambda b, i: (b, nt - 1 - i, j))
    const2 = lambda b, i: (0, 0)
    state_spec = lambda r, c: pl.BlockSpec((nseq, nchunk, r, c), lambda b, i: (b, nt - 1 - i, 0, 0))
    nc = Lp // BLOCK
    return pl.pallas_call(
        functools.partial(_bwd_state_kernel, nchunk=nchunk, nseq=nseq),
        grid=(B // nseq, nt),
        in_specs=[
            col(1), col(2), col(5), col(6),
            pl.BlockSpec((nseq, 8, HW), rev(lambda b, t: (b, jnp.maximum(t * r8 - 1, 0), 5))),
            pl.BlockSpec((nseq, 8, HW), rev(lambda b, t: (b, jnp.minimum((t + 1) * r8, nt * r8 - 1), 5))),
            pl.BlockSpec((nseq, tm, BLOCK), rev(lambda b, t: (b, t, 0))),
            pl.BlockSpec((3, 2 * HW), const2),
            pl.BlockSpec((1, BLOCK), const2),
            pl.BlockSpec((HW, BLOCK), const2),
        ],
        out_specs=[state_spec(HW, HW), state_spec(HW, HW), state_spec(16, HW), state_spec(16, BLOCK)],
        out_shape=[
            jax.ShapeDtypeStruct((B, nc, HW, HW), bf16),
            jax.ShapeDtypeStruct((B, nc, HW, HW), bf16),
            jax.ShapeDtypeStruct((B, nc, 16, HW), bf16),
            jax.ShapeDtypeStruct((B, nc, 16, BLOCK), f32),
        ],
        scratch_shapes=[
            pltpu.VMEM((nseq, HW, HW), f32), pltpu.VMEM((nseq, HW, HW), f32), pltpu.VMEM((nseq, 16, HW), f32),
            pltpu.VMEM((nseq, 16, BLOCK), f32), pltpu.VMEM((HW, BLOCK), f32),
        ],
        compiler_params=pltpu.CompilerParams(
            dimension_semantics=("parallel", "arbitrary"), vmem_limit_bytes=VMEM_LIMIT),
        name="bwd_states",
    )(rest, rest, rest, rest, rest, rest, gates, cw, gb, lgb)


def _fwd_sweep(rest, gates, rb, sb, nbw, mb, cw, gb, lgf, lgb, lgf4, lgb4, rnw, mnw, h, att_o, w_out, tm):
    B, Lp, _ = rest.shape
    nt = Lp // tm
    nchunk = tm // BLOCK
    r8 = tm // 8
    n_tiles = B * nt
    cur = lambda g: jnp.minimum(g, n_tiles - 1)
    done = lambda g: jnp.maximum(g - 1, 0)
    seq = lambda f: f // nt
    til = lambda f: lax.rem(f, nt)
    const2 = lambda g: (0, 0)
    state_spec = lambda r, c: pl.BlockSpec((1, nchunk, r, c), lambda g: (seq(cur(g)), til(cur(g)), 0, 0))
    return pl.pallas_call(
        functools.partial(_fwd_kernel, nchunk=nchunk, nt=nt),
        grid=(n_tiles + 1,),
        in_specs=[
            pl.BlockSpec((1, tm, REST_W), lambda g: (seq(cur(g)), til(cur(g)), 0)),
            pl.BlockSpec((1, 8, 2 * HW), lambda g: (seq(cur(g)), jnp.maximum(til(cur(g)) * r8 - 1, 0), 2)),
            pl.BlockSpec((1, 8, 2 * HW), lambda g: (seq(cur(g)), jnp.minimum((til(cur(g)) + 1) * r8, nt * r8 - 1), 2)),
            pl.BlockSpec((1, tm, BLOCK), lambda g: (seq(cur(g)), til(cur(g)), 0)),
            state_spec(HW, HW), state_spec(HW, HW), state_spec(16, HW), state_spec(16, BLOCK),
            pl.BlockSpec((3, 2 * HW), const2),
            pl.BlockSpec((1, BLOCK), const2),
            pl.BlockSpec((HW, BLOCK), const2),
            pl.BlockSpec((HW, BLOCK), const2),
            pl.BlockSpec((4 * BLOCK, BLOCK), const2),
            pl.BlockSpec((4 * BLOCK, BLOCK), const2),
            pl.BlockSpec((HW, BLOCK), const2),
            pl.BlockSpec((HW, BLOCK), const2),
            pl.BlockSpec((1, tm, D_MODEL), lambda g: (seq(done(g)), til(done(g)), 0)),
            pl.BlockSpec((1, tm, ATT_W), lambda g: (seq(done(g)), til(done(g)), 0)),
            pl.BlockSpec((D_MODEL, D_MODEL), const2),
        ],
        out_specs=pl.BlockSpec((1, tm, D_MODEL), lambda g: (seq(done(g)), til(done(g)), 0)),
        out_shape=jax.ShapeDtypeStruct((B, Lp, D_MODEL), f32),
        scratch_shapes=[
            pltpu.VMEM((HW, HW), f32), pltpu.VMEM((HW, HW), f32), pltpu.VMEM((16, HW), f32),
            pltpu.VMEM((16, BLOCK), f32),
            pltpu.VMEM((HW, BLOCK), f32), pltpu.VMEM((HW, BLOCK), f32), pltpu.VMEM((HW, BLOCK), f32),
            pltpu.VMEM((4 * BLOCK, BLOCK), f32),
            pltpu.VMEM((2, tm, 2 * HW), bf16),
        ],
        compiler_params=pltpu.CompilerParams(
            dimension_semantics=("arbitrary",), vmem_limit_bytes=VMEM_LIMIT),
        name="fwd_sweep",
    )(rest, rest, rest, gates, rb, sb, nbw, mb, cw, gb, lgf, lgb, lgf4, lgb4, rnw, mnw, h, att_o, w_out)


def _ffn_kernel(h_ref, hp_ref, hn_ref, n2_ref, wup_ref, cw_ref, wdn_ref, o_ref, act_ref):
    i = pl.program_id(1)
    last = i == pl.num_programs(1) - 1
    nw = n2_ref[...]

    def norm(x):
        ms = jnp.mean(x * x, axis=-1, keepdims=True)
        return ((x * lax.rsqrt(ms + EPS)) * nw).astype(bf16)

    x = h_ref[0]
    tm = x.shape[0]
    xe = norm(jnp.concatenate([x, hp_ref[0], jnp.where(last, 0.0, hn_ref[0])], axis=0))
    rows = lax.broadcasted_iota(jnp.int32, (tm, 1), 0)
    first_row = rows == 0
    last_row = rows == tm - 1

    def conv(ue, w3):
        u = ue[0:tm]
        up = jnp.where(first_row, ue[tm + 7:tm + 8, :], pltpu.roll(u, 1, 0))
        dn = jnp.where(last_row, ue[tm + 8:tm + 9, :], pltpu.roll(u, tm - 1, 0))
        return (up * w3[0:1, :] + u * w3[1:2, :]) + dn * w3[2:3, :]

    nck = D_FF // FF_CHUNK

    def up(c):
        gs = slice(c * FF_CHUNK, (c + 1) * FF_CHUNK)
        vs = slice(D_FF + c * FF_CHUNK, D_FF + (c + 1) * FF_CHUNK)
        return _dot(xe, wup_ref[:, gs]), _dot(xe, wup_ref[:, vs])

    u_next = up(0)
    acc = None
    done = 0
    for c in range(nck):
        ug, uv = u_next
        if c + 1 < nck:
            u_next = up(c + 1)
        gs = slice(c * FF_CHUNK, (c + 1) * FF_CHUNK)
        vs = slice(D_FF + c * FF_CHUNK, D_FF + (c + 1) * FF_CHUNK)
        g = conv(ug, cw_ref[:, gs])
        v = conv(uv, cw_ref[:, vs])
        act_ref[:, gs] = ((g * _sigmoid(g)) * v).astype(bf16)
        if (c + 1) % DOWN_GROUP == 0 or c + 1 == nck:
            ks = slice(done * FF_CHUNK, (c + 1) * FF_CHUNK)
            part = _dot(act_ref[:, ks], wdn_ref[ks, :])
            acc = part if acc is None else acc + part
            done = c + 1
    y = x + acc
    keep = jnp.logical_or(i > 0, rows >= PAD)
    o_ref[0] = jnp.where(keep, y, 0.0)


def _ffn(h, n2, w_up, cw, w_dn, tm):
    B, Lp, _ = h.shape
    nt = Lp // tm
    r8 = tm // 8
    return pl.pallas_call(
        _ffn_kernel,
        grid=(B, nt),
        in_specs=[
            pl.BlockSpec((1, tm, D_MODEL), lambda b, i: (b, i, 0)),
            pl.BlockSpec((1, 8, D_MODEL), lambda b, i: (b, jnp.maximum(i * r8 - 1, 0), 0)),
            pl.BlockSpec((1, 8, D_MODEL), lambda b, i: (b, jnp.minimum((i + 1) * r8, nt * r8 - 1), 0)),
            pl.BlockSpec((1, D_MODEL), lambda b, i: (0, 0)),
            pl.BlockSpec((D_MODEL, 2 * D_FF), lambda b, i: (0, 0), pipeline_mode=pl.Buffered(1)),
            pl.BlockSpec((3, 2 * D_FF), lambda b, i: (0, 0)),
            pl.BlockSpec((D_FF, D_MODEL), lambda b, i: (0, 0), pipeline_mode=pl.Buffered(1)),
        ],
        out_specs=pl.BlockSpec((1, tm, D_MODEL), lambda b, i: (b, i, 0)),
        out_shape=jax.ShapeDtypeStruct(h.shape, f32),
        scratch_shapes=[pltpu.VMEM((tm, D_FF), bf16)],
        compiler_params=pltpu.CompilerParams(
            dimension_semantics=("parallel", "parallel"), vmem_limit_bytes=VMEM_LIMIT),
        name="ffn",
    )(h, h, h, n2, w_up, cw, w_dn)


def _row_groups(x, height):
    col = jnp.repeat(x, height, axis=-1)
    return jnp.broadcast_to(col[:, :, None], col.shape + (BLOCK,))


def _lane_bcast(w):
    return jnp.broadcast_to(w.astype(f32)[:, :, None], w.shape + (BLOCK,))


def _prepare_params(norm1_w, w_in, attn_q_norm_w, attn_k_norm_w, attn_sink, ret_decay_logit,
                    ret_norm_w, mlstm_conv_w, mlstm_gate_b, mlstm_norm_w, w_out, norm2_w,
                    ffn_up, ffn_conv_w, ffn_down):
    scale = HEAD_DIM ** -0.5
    qrow = jnp.tile(attn_q_norm_w.astype(f32) * (scale * LOG2E), (1, 2))
    krow = jnp.tile(attn_k_norm_w.astype(f32), (1, 2))
    qkw = jnp.concatenate([jnp.stack([qrow] * 4 + [krow], axis=1),
                           jnp.zeros((DEPTH, 3, BLOCK), f32)], axis=1)
    sink = jnp.repeat(attn_sink.astype(f32).reshape(DEPTH, ATT_KV_HEADS, 4) * LOG2E, BLOCK, axis=-1)
    log_gamma = jax.nn.log_sigmoid(ret_decay_logit.astype(f32))
    gb = jnp.concatenate([mlstm_gate_b.astype(f32), jnp.zeros((DEPTH, BLOCK - N_GATES), f32)], axis=-1)
    return dict(
        n1=norm1_w.astype(f32)[:, None, :],
        w_main=w_in[:, :, :MAIN_W].astype(bf16),
        w_gate=jnp.concatenate([w_in[:, :, MAIN_W:], jnp.zeros((DEPTH, D_MODEL, BLOCK - N_GATES), w_in.dtype)],
                               axis=-1).astype(bf16),
        qkw=qkw,
        sink=sink[:, :, None, :],
        lgf=_row_groups(log_gamma[:, 0], HEAD_DIM),
        lgb=_row_groups(log_gamma[:, 1], HEAD_DIM),
        lgf4=_row_groups(log_gamma[:, 0], BLOCK),
        lgb4=_row_groups(log_gamma[:, 1], BLOCK),
        rnw=_lane_bcast(ret_norm_w),
        cw=mlstm_conv_w.astype(f32),
        gb=gb[:, None, :],
        mnw=_lane_bcast(mlstm_norm_w),
        w_out=w_out.astype(bf16),
        n2=norm2_w.astype(f32)[:, None, :],
        w_up=ffn_up.astype(bf16),
        fcw=ffn_conv_w.astype(f32),
        w_dn=ffn_down.astype(bf16),
    )


def _trunk(x, meta_tokens, params, bias):
    B, seq, _ = x.shape
    lp = seq + BLOCK
    tm = _row_tile(lp)
    meta = jnp.broadcast_to(meta_tokens[None].astype(x.dtype), (B, N_META, D_MODEL))
    h = jnp.concatenate([jnp.zeros((B, PAD, D_MODEL), x.dtype), meta, x], axis=1)

    def layer(h, p):
        att, rest, gates = _project(h, p["n1"], p["w_main"], p["w_gate"], p["qkw"], tm)
        att_o = _attention(att, bias, p["sink"], tm)
        rb, sb, nbw, mb = _bwd_states(rest, gates, p["cw"], p["gb"], p["lgb"], tm)
        h = _fwd_sweep(rest, gates, rb, sb, nbw, mb, p["cw"], p["gb"], p["lgf"], p["lgb"],
                       p["lgf4"], p["lgb4"], p["rnw"], p["mnw"], h, att_o, p["w_out"], tm)
        h = _ffn(h, p["n2"], p["w_up"], p["fcw"], p["w_dn"], tm)
        return h, None

    for l in range(DEPTH):
        h, _ = layer(h, {name: value[l] for name, value in params.items()})
    return h[:, BLOCK:]


def kernel(x_prompt, x_sample, meta_tokens, norm1_w, w_in, attn_q_norm_w, attn_k_norm_w, attn_sink,
           ret_decay_logit, ret_norm_w, mlstm_conv_w, mlstm_gate_b, mlstm_norm_w, w_out, norm2_w,
           ffn_up, ffn_conv_w, ffn_down):
    params = _prepare_params(norm1_w, w_in, attn_q_norm_w, attn_k_norm_w, attn_sink, ret_decay_logit,
                             ret_norm_w, mlstm_conv_w, mlstm_gate_b, mlstm_norm_w, w_out, norm2_w,
                             ffn_up, ffn_conv_w, ffn_down)
    bias = _attention_bias()
    y_prompt = _trunk(x_prompt, meta_tokens, params, bias)
    y_sample = _trunk(x_sample, meta_tokens, params, bias)
    return (y_prompt, y_sample)
```
